```python
import jax, jax.numpy as jnp
from jax import lax
import numpy as np

D_MODEL = 2048
BATCH = 2
SEQ = 4096
DEPTH = 4
DEC_BATCH = 8
DEC_SEQ = 8
PAST_LEN = 16384
PAGE_SIZE = 128

N_A_LAYERS = DEPTH // 2
N_B_LAYERS = DEPTH - N_A_LAYERS
D_FF = D_MODEL * 11 // 4
CHUNK = 128
D_U = 2 * D_MODEL
N_GROUPS = 16
HEAD_DIM = 128
N_HEADS = D_MODEL // HEAD_DIM
ATTN_WIDTH = N_HEADS * HEAD_DIM
Q_BLOCK = 128
RMS_EPS = 1e-6
LN_EPS = 1e-5
NEG_INF = -1e30

kernel_name = "yoco_sgu_fox_decoder_step"


def rmsnorm(x, g):
    xf = x.astype(jnp.float32)
    y = xf * lax.rsqrt(jnp.mean(xf * xf, axis=-1, keepdims=True) + RMS_EPS)
    return (y * g.astype(jnp.float32)).astype(x.dtype)


def layernorm(x, g, b):
    xf = x.astype(jnp.float32)
    xc = xf - jnp.mean(xf, axis=-1, keepdims=True)
    var = jnp.mean(xc * xc, axis=-1, keepdims=True)
    return (xc * lax.rsqrt(var + LN_EPS) * g.astype(jnp.float32) + b.astype(jnp.float32)).astype(x.dtype)


def swiglu_sublayer(x, g_in, g_out, wg, wu, wd):
    h = rmsnorm(x, g_in)
    return rmsnorm((jax.nn.silu(h @ wg) * (h @ wu)) @ wd, g_out)


def chunk_sgu_mixer(h, w_in, ln_g, ln_b, w_s, b_s, w_out):
    bsz, s_len, _ = h.shape
    z = jax.nn.gelu(h @ w_in, approximate=False)
    u, v = jnp.split(z, 2, axis=-1)
    v = layernorm(v, ln_g, ln_b)
    n_chunks = -(-s_len // CHUNK)
    pad = n_chunks * CHUNK - s_len
    vc = jnp.pad(v, ((0, 0), (0, pad), (0, 0))).reshape(bsz, n_chunks, CHUNK, N_GROUPS, D_U // N_GROUPS)
    causal = jnp.tril(jnp.ones((CHUNK, CHUNK), dtype=bool))
    w_m = jnp.where(causal[None], w_s, jnp.zeros_like(w_s))
    mixed = jnp.einsum('gts,bnsgd->bntgd', w_m, vc) + jnp.transpose(b_s)[None, None, :, :, None]
    mixed = mixed.reshape(bsz, n_chunks * CHUNK, D_U)[:, :s_len]
    out = (u * mixed) @ w_out
    tail = s_len - ((s_len - 1) // CHUNK) * CHUNK
    return out, v[:, s_len - tail:]


def shared_kv(h, g_kv, w_kvf, b_f):
    bsz, s_len, _ = h.shape
    z = rmsnorm(h, g_kv) @ w_kvf
    k = z[..., :ATTN_WIDTH].reshape(bsz, s_len, N_HEADS, HEAD_DIM)
    v = z[..., ATTN_WIDTH:2 * ATTN_WIDTH].reshape(bsz, s_len, N_HEADS, HEAD_DIM)
    logf = jax.nn.log_sigmoid(z[..., 2 * ATTN_WIDTH:].astype(jnp.float32) + b_f.astype(jnp.float32))
    return k, v, logf


def fox_attention(q, cq, q_pos, segments):
    bsz, t_len, n_h, d_h = q.shape
    qb = min(Q_BLOCK, t_len)
    nb = -(-t_len // qb)
    pad = nb * qb - t_len
    qp = jnp.pad(q, ((0, 0), (0, pad), (0, 0), (0, 0)))
    cqp = jnp.pad(cq, ((0, 0), (0, pad), (0, 0)))
    pp = jnp.pad(q_pos, (0, pad), mode='edge')
    q_blocks = qp.reshape(bsz, nb, qb, n_h, d_h).transpose(1, 0, 2, 3, 4)
    cq_blocks = cqp.reshape(bsz, nb, qb, n_h).transpose(1, 0, 3, 2)
    p_blocks = pp.reshape(nb, qb)
    scale = d_h ** -0.5

    def one_block(args):
        qi, cqi, pi = args
        logits = []
        for k, _, ck_t, kp in segments:
            s = jnp.einsum('bqhd,bkhd->bhqk', qi, k, preferred_element_type=jnp.float32) * scale
            s = s + cqi[..., None] - ck_t[:, :, None, :]
            logits.append(jnp.where((kp[None, :] <= pi[:, None])[None, None], s, NEG_INF))
        probs = jax.nn.softmax(jnp.concatenate(logits, axis=-1), axis=-1)
        out = None
        off = 0
        for k, v, _, _ in segments:
            n = k.shape[1]
            o = jnp.einsum('bhqk,bkhd->bqhd', probs[..., off:off + n].astype(v.dtype), v)
            out = o if out is None else out + o
            off += n
        return out

    o = lax.map(one_block, (q_blocks, cq_blocks, p_blocks))
    return o.transpose(1, 0, 2, 3, 4).reshape(bsz, nb * qb, n_h, d_h)[:, :t_len]


def prompt_context(k, v, logf):
    c = jnp.cumsum(logf, axis=1)
    pos = jnp.arange(k.shape[1], dtype=jnp.int32)
    return c, pos, ((k, v, c.transpose(0, 2, 1), pos),)


def paged_context(k_new, v_new, logf_new, cache_k, cache_v, cache_logf, page_table):
    n_seq, t_len = k_new.shape[:2]
    past = page_table.shape[1] * PAGE_SIZE
    k_past = cache_k[page_table].reshape(n_seq, past, N_HEADS, HEAD_DIM)
    v_past = cache_v[page_table].reshape(n_seq, past, N_HEADS, HEAD_DIM)
    lf_past = cache_logf[page_table].reshape(n_seq, past, N_HEADS).astype(jnp.float32)
    ck_past = lf_past - lax.cumsum(lf_past, axis=1, reverse=True)
    c_new = jnp.cumsum(logf_new, axis=1)
    pos_past = jnp.arange(past, dtype=jnp.int32)
    pos_new = past + jnp.arange(t_len, dtype=jnp.int32)
    segs = ((k_past, v_past, ck_past.transpose(0, 2, 1), pos_past),
            (k_new, v_new, c_new.transpose(0, 2, 1), pos_new))
    return c_new, pos_new, segs


def trunk(x, attn_context, g_pre, g_post, w_ffn_gate, w_ffn_up, w_ffn_down,
          w_a_in, a_ln_g, a_ln_b, w_a_spatial, b_a_spatial, w_a_out,
          g_kv, w_kvf, b_f, w_q, w_o):
    bsz, t_len, _ = x.shape
    sgu_rows = []
    kv_rows = None
    ctx = None
    for l in range(DEPTH):
        if l == N_A_LAYERS:
            kv_rows = shared_kv(x, g_kv, w_kvf, b_f)
            ctx = attn_context(*kv_rows)
        x = x + 0.5 * swiglu_sublayer(x, g_pre[l, 0], g_post[l, 0],
                                      w_ffn_gate[l, 0], w_ffn_up[l, 0], w_ffn_down[l, 0])
        h = rmsnorm(x, g_pre[l, 1])
        if l < N_A_LAYERS:
            m, v_rows = chunk_sgu_mixer(h, w_a_in[l], a_ln_g[l], a_ln_b[l],
                                        w_a_spatial[l], b_a_spatial[l], w_a_out[l])
            sgu_rows.append(v_rows)
        else:
            b = l - N_A_LAYERS
            cq, q_pos, segs = ctx
            q = (h @ w_q[b]).reshape(bsz, t_len, N_HEADS, HEAD_DIM)
            o = fox_attention(q, cq, q_pos, segs)
            m = o.reshape(bsz, t_len, ATTN_WIDTH) @ w_o[b]
        x = x + rmsnorm(m, g_post[l, 1])
        x = x + 0.5 * swiglu_sublayer(x, g_pre[l, 2], g_post[l, 2],
                                      w_ffn_gate[l, 1], w_ffn_up[l, 1], w_ffn_down[l, 1])
    return x, kv_rows, jnp.stack(sgu_rows)


def setup_inputs(seed: int = 0) -> dict:
    key = jax.random.key(seed)
    ks = jax.random.split(key, 24)
    n_pages = PAST_LEN // PAGE_SIZE
    n_used = DEC_BATCH * n_pages
    n_pool = n_used + max(n_used // 4, 1)

    def nrm(k, shape, fan_in):
        return jax.random.normal(k, shape, jnp.float32) * (fan_in ** -0.5)

    def gain(k, shape):
        return 1.0 + 0.1 * jax.random.normal(k, shape, jnp.float32)

    x_prompt = jax.random.normal(ks[0], (BATCH, SEQ, D_MODEL), jnp.float32)
    x_sample = jax.random.normal(ks[1], (DEC_BATCH, DEC_SEQ, D_MODEL), jnp.float32)
    cache_k = jax.random.normal(ks[2], (n_pool, PAGE_SIZE, N_HEADS, HEAD_DIM), jnp.float32)
    cache_v = jax.random.normal(ks[3], (n_pool, PAGE_SIZE, N_HEADS, HEAD_DIM), jnp.float32)
    cache_logf = jax.nn.log_sigmoid(3.5 + 1.5 * jax.random.normal(ks[4], (n_pool, PAGE_SIZE, N_HEADS), jnp.float32))
    page_table = jax.random.permutation(ks[5], n_pool)[:n_used].reshape(DEC_BATCH, n_pages).astype(jnp.int32)
    g_pre = gain(ks[6], (DEPTH, 3, D_MODEL))
    g_post = gain(ks[7], (DEPTH, 3, D_MODEL))
    w_ffn_gate = nrm(ks[8], (DEPTH, 2, D_MODEL, D_FF), D_MODEL)
    w_ffn_up = nrm(ks[9], (DEPTH, 2, D_MODEL, D_FF), D_MODEL)
    w_ffn_down = nrm(ks[10], (DEPTH, 2, D_FF, D_MODEL), D_FF)
    w_a_in = nrm(ks[11], (N_A_LAYERS, D_MODEL, 2 * D_U), D_MODEL)
    a_ln_g = gain(ks[12], (N_A_LAYERS, D_U))
    a_ln_b = 0.02 * jax.random.normal(ks[13], (N_A_LAYERS, D_U), jnp.float32)
    w_a_spatial = nrm(ks[14], (N_A_LAYERS, N_GROUPS, CHUNK, CHUNK), CHUNK)
    b_a_spatial = gain(ks[15], (N_A_LAYERS, N_GROUPS, CHUNK))
    w_a_out = nrm(ks[16], (N_A_LAYERS, D_U, D_MODEL), D_U)
    g_kv = gain(ks[17], (D_MODEL,))
    w_kvf = nrm(ks[18], (D_MODEL, 2 * ATTN_WIDTH + N_HEADS), D_MODEL)
    b_f = jax.random.uniform(ks[19], (N_HEADS,), jnp.float32, 1.0, 6.0)
    w_q = nrm(ks[20], (N_B_LAYERS, D_MODEL, ATTN_WIDTH), D_MODEL)
    w_o = nrm(ks[21], (N_B_LAYERS, ATTN_WIDTH, D_MODEL), ATTN_WIDTH)
    return {"x_prompt": x_prompt, "x_sample": x_sample,
            "cache_k": cache_k, "cache_v": cache_v, "cache_logf": cache_logf, "page_table": page_table,
            "g_pre": g_pre, "g_post": g_post,
            "w_ffn_gate": w_ffn_gate, "w_ffn_up": w_ffn_up, "w_ffn_down": w_ffn_down,
            "w_a_in": w_a_in, "a_ln_g": a_ln_g, "a_ln_b": a_ln_b,
            "w_a_spatial": w_a_spatial, "b_a_spatial": b_a_spatial, "w_a_out": w_a_out,
            "g_kv": g_kv, "w_kvf": w_kvf, "b_f": b_f, "w_q": w_q, "w_o": w_o}


def reference(x_prompt, x_sample, cache_k, cache_v, cache_logf, page_table,
              g_pre, g_post, w_ffn_gate, w_ffn_up, w_ffn_down,
              w_a_in, a_ln_g, a_ln_b, w_a_spatial, b_a_spatial, w_a_out,
              g_kv, w_kvf, b_f, w_q, w_o):
    weights = (g_pre, g_post, w_ffn_gate, w_ffn_up, w_ffn_down,
               w_a_in, a_ln_g, a_ln_b, w_a_spatial, b_a_spatial, w_a_out,
               g_kv, w_kvf, b_f, w_q, w_o)

    def sample_context(k, v, logf):
        return paged_context(k, v, logf, cache_k, cache_v, cache_logf, page_table)

    y_prompt, (k_p, v_p, lf_p), sgu_v_prompt = trunk(x_prompt, prompt_context, *weights)
    y_sample, (k_s, v_s, lf_s), sgu_v_sample = trunk(x_sample, sample_context, *weights)
    return (y_prompt, y_sample, k_p, v_p, lf_p, k_s, v_s, lf_s, sgu_v_prompt, sgu_v_sample)
```

```python
import functools

import jax
import jax.numpy as jnp
from jax import lax
from jax.experimental import pallas as pl
from jax.experimental.pallas import tpu as pltpu

F32 = jnp.float32
BF16 = jnp.bfloat16

RMS_EPS = 1e-6
LN_EPS = 1e-5
NEG_INF = -1e30
CHUNK = 128
N_GROUPS = 16
HEAD_DIM = 128
PAGE = 128

V7X_VMEM_LIMIT_CAP = 56 * 1024 * 1024
LANES = 128


def _vmem_limit(estimate_bytes):
    return int(min(max(estimate_bytes * 5 // 4, 16 * 1024 * 1024), V7X_VMEM_LIMIT_CAP))


def _params(estimate_bytes, n_grid_dims):
    return pltpu.CompilerParams(
        dimension_semantics=("arbitrary",) * n_grid_dims,
        vmem_limit_bytes=_vmem_limit(estimate_bytes))


def _rms(x, g):
    return x * lax.rsqrt(jnp.mean(x * x, axis=-1, keepdims=True) + RMS_EPS) * g


def _dot(a, b):
    return jnp.dot(a, b, preferred_element_type=F32)


def _dot_exact(a, b):
    return jnp.dot(a, b, preferred_element_type=F32, precision=lax.Precision.HIGHEST)


def _prenorm_body(xp_ref, xs_ref, g_ref, hp_ref, hs_ref, *, n_tiles):
    i = pl.program_id(0)

    @pl.when(i < n_tiles)
    def _():
        hp_ref[...] = _rms(xp_ref[...], g_ref[...]).astype(BF16)

    @pl.when(i == n_tiles)
    def _():
        hs_ref[...] = _rms(xs_ref[...], g_ref[...]).astype(BF16)


def _prenorm(x_p, x_s, g, bm=512):
    mp, d = x_p.shape
    ms = x_s.shape[0]
    n_tiles = mp // bm
    last = n_tiles - 1
    row = lambda i: (jnp.minimum(i, last), 0)
    fixed = lambda i: (0, 0)
    return pl.pallas_call(
        functools.partial(_prenorm_body, n_tiles=n_tiles),
        grid=(n_tiles + 1,),
        in_specs=[pl.BlockSpec((bm, d), row), pl.BlockSpec((ms, d), fixed),
                  pl.BlockSpec((1, d), fixed)],
        out_specs=[pl.BlockSpec((bm, d), row), pl.BlockSpec((ms, d), fixed)],
        out_shape=[jax.ShapeDtypeStruct((mp, d), BF16), jax.ShapeDtypeStruct((ms, d), BF16)],
        compiler_params=_params(6 * bm * d * 4, 1),
        name="prenorm",
    )(x_p, x_s, g.reshape(1, d))


def _mm_ws_body(*refs, n_w, n_out, n_tiles, epilogue):
    hp_ref, hs_ref = refs[0], refs[1]
    w_refs = refs[2:2 + n_w]
    out_refs = refs[2 + n_w:2 + n_w + 2 * n_out]
    wbf_refs = refs[2 + n_w + 2 * n_out:]
    m = pl.program_id(1)

    @pl.when(m == 0)
    def _():
        for w_ref, wbf_ref in zip(w_refs, wbf_refs):
            wbf_ref[...] = w_ref[...].astype(BF16)

    def run(h_ref, o_refs):
        h = h_ref[...].astype(BF16)
        outs = epilogue(*[_dot(h, wbf_ref[...]) for wbf_ref in wbf_refs])
        for o_ref, o in zip(o_refs, outs):
            o_ref[...] = o.astype(o_ref.dtype)

    @pl.when(m < n_tiles)
    def _():
        run(hp_ref, out_refs[0::2])

    @pl.when(m == n_tiles)
    def _():
        run(hs_ref, out_refs[1::2])


def _mm_ws(h_p, h_s, weights, n_cols, epilogue, out_dtypes, *, bm=1024, bn=512, name):
    mp, k = h_p.shape
    ms = h_s.shape[0]
    n_tiles = mp // bm
    last = n_tiles - 1
    n_w, n_out = len(weights), len(out_dtypes)
    in_specs = [pl.BlockSpec((bm, k), lambda n, m: (jnp.minimum(m, last), 0)),
                pl.BlockSpec((ms, k), lambda n, m: (0, 0))]
    for _, off in weights:
        assert off % bn == 0
        in_specs.append(pl.BlockSpec((k, bn), lambda n, m, o=off // bn: (0, n + o)))
    out_specs, out_shape = [], []
    for dt_p, dt_s in out_dtypes:
        out_specs += [pl.BlockSpec((bm, bn), lambda n, m: (jnp.minimum(m, last), n)),
                      pl.BlockSpec((ms, bn), lambda n, m: (0, n))]
        out_shape += [jax.ShapeDtypeStruct((mp, n_cols), dt_p),
                      jax.ShapeDtypeStruct((ms, n_cols), dt_s)]
    est = (2 * bm * k * 2 + n_w * k * bn * (2 * 4 + 2)
           + n_out * 2 * bm * bn * 4 + (n_w + n_out) * bm * bn * 4)
    return pl.pallas_call(
        functools.partial(_mm_ws_body, n_w=n_w, n_out=n_out, n_tiles=n_tiles, epilogue=epilogue),
        grid=(n_cols // bn, n_tiles + 1),
        in_specs=in_specs, out_specs=out_specs, out_shape=out_shape,
        scratch_shapes=[pltpu.VMEM((k, bn), BF16) for _ in weights],
        compiler_params=_params(est, 2),
        name=name,
    )(h_p, h_s, *[w for w, _ in weights])


def _swiglu_epilogue(g, u):
    return (g / (1.0 + jnp.exp(-g)) * u,)


def _gelu_epilogue(z):
    return (0.5 * z * (1.0 + lax.erf(z * (2.0 ** -0.5))),)


def _identity2_epilogue(z):
    return (z, z)


def _identity_epilogue(z):
    return (z,)


def _mm_res_body(*refs, n_h, n_tiles, coef):
    ap_ref, as_ref, w_ref, xp_ref, xs_ref, gpost_ref, gnext_ref = refs[:7]
    out_refs = refs[7:]
    i = pl.program_id(0)

    def run(a_ref, x_ref, o_refs):
        y = _dot(a_ref[...].astype(BF16), w_ref[...])
        x_new = x_ref[...] + coef * _rms(y, gpost_ref[...])
        o_refs[0][...] = x_new
        if n_h:
            inv = lax.rsqrt(jnp.mean(x_new * x_new, axis=-1, keepdims=True) + RMS_EPS)
            xn = x_new * inv
            for j in range(n_h):
                o_refs[1 + j][...] = (xn * gnext_ref[j:j + 1, :]).astype(BF16)

    @pl.when(i < n_tiles)
    def _():
        run(ap_ref, xp_ref, out_refs[0::2])

    @pl.when(i == n_tiles)
    def _():
        run(as_ref, xs_ref, out_refs[1::2])


def _mm_res(a_p, a_s, w_bf, x_p, x_s, g_post, g_next, coef, *, bm, name):
    mp, k = a_p.shape
    ms = a_s.shape[0]
    d = w_bf.shape[1]
    n_h = 0 if g_next is None else g_next.shape[0]
    g_next_arr = jnp.zeros((1, d), F32) if g_next is None else g_next
    n_tiles = mp // bm
    last = n_tiles - 1
    row = lambda i: (jnp.minimum(i, last), 0)
    fixed = lambda i: (0, 0)
    in_specs = [pl.BlockSpec((bm, k), row), pl.BlockSpec((ms, k), fixed),
                pl.BlockSpec((k, d), fixed, pipeline_mode=pl.Buffered(1)),
                pl.BlockSpec((bm, d), row), pl.BlockSpec((ms, d), fixed),
                pl.BlockSpec((1, d), fixed), pl.BlockSpec(g_next_arr.shape, fixed)]
    out_specs = [pl.BlockSpec((bm, d), row), pl.BlockSpec((ms, d), fixed)]
    out_shape = [jax.ShapeDtypeStruct((mp, d), F32), jax.ShapeDtypeStruct((ms, d), F32)]
    for _ in range(n_h):
        out_specs += [pl.BlockSpec((bm, d), row), pl.BlockSpec((ms, d), fixed)]
        out_shape += [jax.ShapeDtypeStruct((mp, d), BF16), jax.ShapeDtypeStruct((ms, d), BF16)]
    est = (k * d * 2 + 2 * bm * k * a_p.dtype.itemsize + 4 * bm * d * 4
           + n_h * 2 * bm * d * 2 + 3 * bm * d * 4)
    return pl.pallas_call(
        functools.partial(_mm_res_body, n_h=n_h, n_tiles=n_tiles, coef=coef),
        grid=(n_tiles + 1,),
        in_specs=in_specs, out_specs=out_specs, out_shape=out_shape,
        compiler_params=_params(est, 1),
        name=name,
    )(a_p, a_s, w_bf, x_p, x_s, g_post.reshape(1, d), g_next_arr)


def _sgu_body(up_ref, vp_ref, us_ref, vs_ref, lng_ref, lnb_ref, wp_ref, bp_ref, ws_ref, bs_ref,
              pp_ref, ps_ref, vtp_ref, vts_ref, *, n_chunks):
    c = pl.program_id(0)

    def run(u_ref, v_ref, w_ref, b_ref, p_ref, vt_ref):
        rows = v_ref.shape[0]
        gw = v_ref.shape[1] // N_GROUPS
        v = v_ref[...]
        xc = v - jnp.mean(v, axis=-1, keepdims=True)
        var = jnp.mean(xc * xc, axis=-1, keepdims=True)
        vn = xc * lax.rsqrt(var + LN_EPS) * lng_ref[...] + lnb_ref[...]
        vt_ref[...] = vn
        vb = vn.astype(BF16)
        causal = (lax.broadcasted_iota(jnp.int32, (rows, rows), 0)
                  >= lax.broadcasted_iota(jnp.int32, (rows, rows), 1))
        for g in range(N_GROUPS):
            wm = jnp.where(causal, w_ref[g], 0.0).astype(BF16)
            mixed = _dot(wm, vb[:, g * gw:(g + 1) * gw]) + b_ref[:, g:g + 1]
            p_ref[:, g * gw:(g + 1) * gw] = (
                u_ref[:, g * gw:(g + 1) * gw].astype(F32) * mixed).astype(BF16)

    @pl.when(c < n_chunks)
    def _():
        run(up_ref, vp_ref, wp_ref, bp_ref, pp_ref, vtp_ref)

    @pl.when(c == n_chunks)
    def _():
        run(us_ref, vs_ref, ws_ref, bs_ref, ps_ref, vts_ref)


def _sgu_mix(u_p, v_p, u_s, v_s, ln_g, ln_b, w_sp, b_sp, batch, dec_batch):
    mp, du = u_p.shape
    ms = u_s.shape[0]
    n_chunks = mp // CHUNK
    chunks_per_seq = n_chunks // batch
    dec_seq = ms // dec_batch
    last = n_chunks - 1
    eye = jnp.eye(dec_batch, dtype=F32)
    w_s = (eye[None, :, None, :, None] * w_sp[:, None, :dec_seq, None, :dec_seq]
           ).reshape(N_GROUPS, ms, ms)
    b_s = jnp.tile(jnp.transpose(b_sp[:, :dec_seq]), (dec_batch, 1))
    b_p = jnp.transpose(b_sp)
    row = lambda c: (jnp.minimum(c, last), 0)
    fixed2 = lambda c: (0, 0)
    fixed3 = lambda c: (0, 0, 0)
    est = 2 * CHUNK * du * (2 + 4 + 2 + 4) + 8 * CHUNK * du * 4
    return pl.pallas_call(
        functools.partial(_sgu_body, n_chunks=n_chunks),
        grid=(n_chunks + 1,),
        in_specs=[pl.BlockSpec((CHUNK, du), row), pl.BlockSpec((CHUNK, du), row),
                  pl.BlockSpec((ms, du), fixed2), pl.BlockSpec((ms, du), fixed2),
                  pl.BlockSpec((1, du), fixed2), pl.BlockSpec((1, du), fixed2),
                  pl.BlockSpec((N_GROUPS, CHUNK, CHUNK), fixed3), pl.BlockSpec((CHUNK, N_GROUPS), fixed2),
                  pl.BlockSpec((N_GROUPS, ms, ms), fixed3), pl.BlockSpec((ms, N_GROUPS), fixed2)],
        out_specs=[pl.BlockSpec((CHUNK, du), row), pl.BlockSpec((ms, du), fixed2),
                   pl.BlockSpec((CHUNK, du), lambda c: (jnp.minimum(c, last) // chunks_per_seq, 0)),
                   pl.BlockSpec((ms, du), fixed2)],
        out_shape=[jax.ShapeDtypeStruct((mp, du), BF16), jax.ShapeDtypeStruct((ms, du), BF16),
                   jax.ShapeDtypeStruct((batch * CHUNK, du), F32), jax.ShapeDtypeStruct((ms, du), F32)],
        compiler_params=_params(est, 1),
        name="sgu_mix",
    )(u_p, v_p, u_s, v_s, ln_g.reshape(1, du), ln_b.reshape(1, du), w_sp, b_p, w_s, b_s)


def _logf_body(xp_ref, xs_ref, g_ref, w_ref, b_ref, lp_ref, ls_ref, *, n_tiles):
    i = pl.program_id(0)

    def run(x_ref, o_ref):
        z = _dot_exact(_rms(x_ref[...], g_ref[...]), w_ref[...]) + b_ref[...]
        o_ref[...] = jnp.minimum(z, 0.0) - jnp.log1p(jnp.exp(-jnp.abs(z)))

    @pl.when(i < n_tiles)
    def _():
        run(xp_ref, lp_ref)

    @pl.when(i == n_tiles)
    def _():
        run(xs_ref, ls_ref)


def _logf(x_p, x_s, g_kv, w_f, b_f, bm=512):
    mp, d = x_p.shape
    ms = x_s.shape[0]
    nh = w_f.shape[1]
    w_pad = jnp.pad(w_f, ((0, 0), (0, LANES - nh)))
    b_pad = jnp.pad(b_f, (0, LANES - nh)).reshape(1, LANES)
    n_tiles = mp // bm
    last = n_tiles - 1
    row = lambda i: (jnp.minimum(i, last), 0)
    fixed = lambda i: (0, 0)
    return pl.pallas_call(
        functools.partial(_logf_body, n_tiles=n_tiles),
        grid=(n_tiles + 1,),
        in_specs=[pl.BlockSpec((bm, d), row), pl.BlockSpec((ms, d), fixed),
                  pl.BlockSpec((1, d), fixed), pl.BlockSpec((d, LANES), fixed),
                  pl.BlockSpec((1, LANES), fixed)],
        out_specs=[pl.BlockSpec((bm, LANES), row), pl.BlockSpec((ms, LANES), fixed)],
        out_shape=[jax.ShapeDtypeStruct((mp, LANES), F32), jax.ShapeDtypeStruct((ms, LANES), F32)],
        compiler_params=_params(8 * bm * d * 4, 1),
        name="logf",
    )(x_p, x_s, g_kv.reshape(1, d), w_pad, b_pad)


def _cumsum_body(l_ref, c_ref, *, n_chunks):
    tril = (lax.broadcasted_iota(jnp.int32, (CHUNK, CHUNK), 0)
            >= lax.broadcasted_iota(jnp.int32, (CHUNK, CHUNK), 1)).astype(F32)

    def step(j, carry):
        r0 = pl.multiple_of(j * CHUNK, CHUNK)
        local = _dot_exact(tril, l_ref[pl.ds(r0, CHUNK), :]) + carry
        c_ref[pl.ds(r0, CHUNK), :] = local
        return local[CHUNK - 1:CHUNK, :]

    lax.fori_loop(0, n_chunks, step, jnp.zeros((1, LANES), F32))


def _cumsum_rows(lf_p, batch):
    mp = lf_p.shape[0]
    seq = mp // batch
    return pl.pallas_call(
        functools.partial(_cumsum_body, n_chunks=seq // CHUNK),
        grid=(batch,),
        in_specs=[pl.BlockSpec((seq, LANES), lambda b: (b, 0))],
        out_specs=pl.BlockSpec((seq, LANES), lambda b: (b, 0)),
        out_shape=jax.ShapeDtypeStruct((mp, LANES), F32),
        compiler_params=_params(4 * seq * LANES * 4, 1),
        name="cumsum_logf",
    )(lf_p)


def _fox_prompt_body(q_ref, k_ref, v_ref, cq_ref, ck_ref, o_ref, *, tq, tk, scale):
    h = pl.program_id(1)
    qi = pl.program_id(2)
    q = q_ref[...]
    lane_is_h = lax.broadcasted_iota(jnp.int32, (1, LANES), 1) == h
    cq = jnp.sum(jnp.where(lane_is_h, cq_ref[...], 0.0), axis=-1, keepdims=True)

    def tile(j, carry, masked):
        m, l, acc = carry
        r0 = pl.multiple_of(j * tk, tk)
        kt = k_ref[pl.ds(r0, tk), :]
        s = lax.dot_general(q, kt, (((1,), (1,)), ((), ())), preferred_element_type=F32) * scale
        s = s + cq - ck_ref[0, j]
        if masked:
            rows = qi * tq + lax.broadcasted_iota(jnp.int32, (tq, tk), 0)
            cols = j * tk + lax.broadcasted_iota(jnp.int32, (tq, tk), 1)
            s = jnp.where(cols <= rows, s, NEG_INF)
        m_new = jnp.maximum(m, jnp.max(s, axis=-1, keepdims=True))
        alpha = jnp.exp(m - m_new)
        p = jnp.exp(s - m_new)
        l = alpha * l + jnp.sum(p, axis=-1, keepdims=True)
        acc = alpha * acc + _dot(p.astype(BF16), v_ref[pl.ds(r0, tk), :])
        return m_new, l, acc

    init = (jnp.full((tq, 1), NEG_INF, F32), jnp.zeros((tq, 1), F32), jnp.zeros((tq, HEAD_DIM), F32))
    n_full = (qi * tq) // tk
    carry = lax.fori_loop(0, n_full, lambda j, c: tile(j, c, False), init)
    for d in range(tq // tk):
        carry = tile(n_full + d, carry, True)
    _, l, acc = carry
    o_ref[...] = (acc / l).astype(o_ref.dtype)


def _fox_prompt(q_p, k_bf, v_bf, c_p, batch, n_heads, tq=512, tk=512):
    mp, width = q_p.shape
    seq = mp // batch
    nq, nk = seq // tq, seq // tk
    c_rows = jnp.transpose(c_p[:, :n_heads].reshape(batch, seq, n_heads), (0, 2, 1))
    c_rows = c_rows.reshape(batch * n_heads, nk, 1, tk)
    est = 2 * (2 * seq * HEAD_DIM * 2) + 8 * tq * tk * 4
    return pl.pallas_call(
        functools.partial(_fox_prompt_body, tq=tq, tk=tk, scale=HEAD_DIM ** -0.5),
        grid=(batch, n_heads, nq),
        in_specs=[pl.BlockSpec((tq, HEAD_DIM), lambda b, h, i: (b * nq + i, h)),
                  pl.BlockSpec((seq, HEAD_DIM), lambda b, h, i: (b, h)),
                  pl.BlockSpec((seq, HEAD_DIM), lambda b, h, i: (b, h)),
                  pl.BlockSpec((tq, LANES), lambda b, h, i: (b * nq + i, 0)),
                  pl.BlockSpec((1, nk, 1, tk), lambda b, h, i: (b * n_heads + h, 0, 0, 0))],
        out_specs=pl.BlockSpec((tq, HEAD_DIM), lambda b, h, i: (b * nq + i, h)),
        out_shape=jax.ShapeDtypeStruct((mp, width), BF16),
        compiler_params=_params(est, 3),
        name="fox_prompt",
    )(q_p, k_bf, v_bf, c_p, c_rows)


def _ck_past_body(pt_ref, *refs, n_group, n_heads, dec_seq):
    lf_refs = refs[:n_group]
    out_ref, carry_ref = refs[n_group:]
    p = pl.program_id(1)

    @pl.when(p == 0)
    def _():
        carry_ref[...] = jnp.zeros_like(carry_ref)

    expand = (lax.broadcasted_iota(jnp.int32, (n_heads, LANES), 1) // dec_seq
              == lax.broadcasted_iota(jnp.int32, (n_heads, LANES), 0)).astype(F32)
    later = (lax.broadcasted_iota(jnp.int32, (PAGE, PAGE), 1)
             > lax.broadcasted_iota(jnp.int32, (PAGE, PAGE), 0)).astype(F32)
    carry = carry_ref[...]
    for g in range(n_group):
        lf_e = _dot_exact(lf_refs[g][0], expand)
        within = _dot_exact(later, lf_e)
        out_ref[0, n_group - 1 - g] = -(within + carry)
        carry = carry + within[0:1, :] + lf_e[0:1, :]
    carry_ref[...] = carry


def _ck_past(cache_logf, page_table, dec_seq, n_group=8):
    n_seq, n_pages = page_table.shape
    n_heads = cache_logf.shape[2]
    assert n_heads * dec_seq == LANES and n_pages % n_group == 0
    lf_map = lambda b, p, pt, g: (pt[b, n_pages - 1 - (p * n_group + g)], 0, 0)
    n_steps = n_pages // n_group
    grid_spec = pltpu.PrefetchScalarGridSpec(
        num_scalar_prefetch=1,
        grid=(n_seq, n_steps),
        in_specs=[pl.BlockSpec((1, PAGE, n_heads), functools.partial(lf_map, g=g))
                  for g in range(n_group)],
        out_specs=pl.BlockSpec((1, n_group, PAGE, LANES), lambda b, p, pt: (b, n_steps - 1 - p, 0, 0)),
        scratch_shapes=[pltpu.VMEM((1, LANES), F32)])
    return pl.pallas_call(
        functools.partial(_ck_past_body, n_group=n_group, n_heads=n_heads, dec_seq=dec_seq),
        grid_spec=grid_spec,
        out_shape=jax.ShapeDtypeStruct((n_seq, n_pages, PAGE, LANES), F32),
        compiler_params=_params(64 * PAGE * LANES * 4, 2),
        name="ck_past",
    )(page_table, *([cache_logf] * n_group))


def _fox_sample_body(pt_ref, *refs, n_group, n_heads, dec_seq, scale):
    k_refs = refs[:n_group]
    v_refs = refs[n_group:2 * n_group]
    ck_ref, qbd_ref, kn_ref, vn_ref, lfn_ref, o_ref, m_ref, l_ref, acc_ref, cq_ref = refs[2 * n_group:]
    p = pl.program_id(1)
    n_steps = pl.num_programs(1)
    qbd = qbd_ref[0]
    eye = (lax.broadcasted_iota(jnp.int32, (LANES, LANES), 0)
           == lax.broadcasted_iota(jnp.int32, (LANES, LANES), 1))

    def to_column(row):
        return jnp.sum(jnp.where(eye, row, 0.0), axis=-1, keepdims=True)

    @pl.when(p == 0)
    def _():
        expand = (lax.broadcasted_iota(jnp.int32, (LANES, LANES), 1) // dec_seq
                  == lax.broadcasted_iota(jnp.int32, (LANES, LANES), 0)).astype(F32)
        lf_e = _dot_exact(lfn_ref[...], expand)
        rows = [lf_e[0:1, :]]
        for j in range(1, dec_seq):
            rows.append(rows[-1] + lf_e[j:j + 1, :])
        c_new = jnp.concatenate(rows, axis=0)
        key_j = lax.broadcasted_iota(jnp.int32, (dec_seq, LANES), 0)
        col_t = lax.broadcasted_iota(jnp.int32, (dec_seq, LANES), 1) % dec_seq
        cq = jnp.sum(jnp.where(key_j == col_t, c_new, 0.0), axis=0, keepdims=True)
        cq_ref[...] = cq
        s = _dot(kn_ref[...].astype(BF16), qbd) * scale + cq - c_new
        s = jnp.where(key_j <= col_t, s, NEG_INF)
        m = jnp.max(s, axis=0, keepdims=True)
        pr = jnp.exp(s - m)
        m_ref[...] = m
        l_ref[...] = jnp.sum(pr, axis=0, keepdims=True)
        acc_ref[...] = lax.dot_general(pr.astype(BF16), vn_ref[...].astype(BF16),
                                       (((0,), (0,)), ((), ())), preferred_element_type=F32)

    cq = cq_ref[...]
    s_pages = [_dot(k_refs[g][0].astype(BF16), qbd) * scale + cq - ck_ref[0, n_group - 1 - g]
               for g in range(n_group)]
    s = jnp.concatenate(s_pages, axis=0)
    m_old = m_ref[...]
    m_new = jnp.maximum(m_old, jnp.max(s, axis=0, keepdims=True))
    alpha = jnp.exp(m_old - m_new)
    pr = jnp.exp(s - m_new)
    m_ref[...] = m_new
    l_ref[...] = alpha * l_ref[...] + jnp.sum(pr, axis=0, keepdims=True)
    v = jnp.concatenate([v_refs[g][0].astype(BF16) for g in range(n_group)], axis=0)
    pv = lax.dot_general(pr.astype(BF16), v, (((0,), (0,)), ((), ())), preferred_element_type=F32)
    acc_ref[...] = to_column(alpha) * acc_ref[...] + pv

    @pl.when(p == n_steps - 1)
    def _():
        inv_l = 1.0 / to_column(l_ref[...])
        for h in range(n_heads):
            r = slice(h * dec_seq, (h + 1) * dec_seq)
            c = slice(h * HEAD_DIM, (h + 1) * HEAD_DIM)
            o_ref[:, c] = acc_ref[r, c] * inv_l[r, :]


def _fox_sample(q_s, k_s, v_s, lf_s, cache_k, cache_v, ck_pages, page_table, dec_seq, n_group=4):
    n_seq, n_pages = page_table.shape
    n_heads = cache_k.shape[2]
    width = n_heads * HEAD_DIM
    ck = cache_k.reshape(cache_k.shape[0], PAGE, width)
    cv = cache_v.reshape(cache_v.shape[0], PAGE, width)
    q_t = jnp.transpose(q_s.reshape(n_seq, dec_seq, n_heads, HEAD_DIM), (0, 2, 3, 1))
    qbd = (q_t[:, :, :, None, :] * jnp.eye(n_heads, dtype=F32)[None, :, None, :, None])
    qbd = qbd.reshape(n_seq, width, n_heads * dec_seq).astype(BF16)
    page_map = lambda b, p, pt, g: (pt[b, n_pages - 1 - (p * n_group + g)], 0, 0)
    n_steps = n_pages // n_group
    seq_map = lambda b, p, pt: (b, 0)
    in_specs = ([pl.BlockSpec((1, PAGE, width), functools.partial(page_map, g=g)) for g in range(n_group)]
                + [pl.BlockSpec((1, PAGE, width), functools.partial(page_map, g=g)) for g in range(n_group)]
                + [pl.BlockSpec((1, n_group, PAGE, LANES), lambda b, p, pt: (b, n_steps - 1 - p, 0, 0)),
                   pl.BlockSpec((1, width, LANES), lambda b, p, pt: (b, 0, 0)),
                   pl.BlockSpec((dec_seq, width), seq_map), pl.BlockSpec((dec_seq, width), seq_map),
                   pl.BlockSpec((dec_seq, LANES), seq_map)])
    grid_spec = pltpu.PrefetchScalarGridSpec(
        num_scalar_prefetch=1,
        grid=(n_seq, n_steps),
        in_specs=in_specs,
        out_specs=pl.BlockSpec((dec_seq, width), seq_map),
        scratch_shapes=[pltpu.VMEM((1, LANES), F32), pltpu.VMEM((1, LANES), F32),
                        pltpu.VMEM((LANES, width), F32), pltpu.VMEM((1, LANES), F32)])
    est = 2 * 2 * n_group * PAGE * width * 4 + n_group * PAGE * width * 8 + 4 * LANES * width * 4
    return pl.pallas_call(
        functools.partial(_fox_sample_body, n_group=n_group, n_heads=n_heads, dec_seq=dec_seq,
                          scale=HEAD_DIM ** -0.5),
        grid_spec=grid_spec,
        out_shape=jax.ShapeDtypeStruct((n_seq * dec_seq, width), F32),
        compiler_params=_params(est, 2),
        name="fox_sample",
    )(page_table, *([ck] * n_group), *([cv] * n_group), ck_pages, qbd, k_s, v_s, lf_s)


def kernel(x_prompt, x_sample, cache_k, cache_v, cache_logf, page_table, g_pre, g_post, w_ffn_gate, w_ffn_up, w_ffn_down, w_a_in, a_ln_g, a_ln_b, w_a_spatial, b_a_spatial, w_a_out, g_kv, w_kvf, b_f, w_q, w_o):
    batch, seq, d_model = x_prompt.shape
    dec_batch, dec_seq, _ = x_sample.shape
    depth = g_pre.shape[0]
    n_a = w_a_in.shape[0]
    d_ff = w_ffn_gate.shape[-1]
    d_u = a_ln_g.shape[-1]
    n_heads = b_f.shape[0]
    width = n_heads * HEAD_DIM

    x_p = x_prompt.reshape(batch * seq, d_model)
    x_s = x_sample.reshape(dec_batch * dec_seq, d_model)
    h_p, h_s = _prenorm(x_p, x_s, g_pre[0, 0])

    def ffn(x_p, x_s, h_p, h_s, l, j, g_next):
        a_p, a_s = _mm_ws(h_p, h_s, [(w_ffn_gate[l, j], 0), (w_ffn_up[l, j], 0)], d_ff,
                          _swiglu_epilogue, [(BF16, BF16)], name="ffn_gate_up")
        return _mm_res(a_p, a_s, w_ffn_down[l, j].astype(BF16), x_p, x_s, g_post[l, 2 * j], g_next,
                       0.5, bm=256, name="ffn_down")

    sgu_p, sgu_s = [], []
    kv_out = None
    for l in range(depth):
        if l == n_a:
            k_p, k_s, kb_p, _ = _mm_ws(hk_p, hk_s, [(w_kvf, 0)], width, _identity2_epilogue,
                                       [(F32, F32), (BF16, BF16)], name="proj_k")
            v_p, v_s, vb_p, _ = _mm_ws(hk_p, hk_s, [(w_kvf, width)], width, _identity2_epilogue,
                                       [(F32, F32), (BF16, BF16)], name="proj_v")
            lf_p, lf_s = _logf(x_p, x_s, g_kv, w_kvf[:, 2 * width:], b_f)
            c_p = _cumsum_rows(lf_p, batch)
            ck_pages = _ck_past(cache_logf, page_table, dec_seq)
            kv_out = (k_p, k_s, v_p, v_s, lf_p, lf_s)

        x_p, x_s, h_p, h_s = ffn(x_p, x_s, h_p, h_s, l, 0, g_pre[l, 1:2])

        if l < n_a:
            u_p, u_s = _mm_ws(h_p, h_s, [(w_a_in[l], 0)], d_u, _gelu_epilogue, [(BF16, BF16)],
                              name="sgu_in_u")
            v_p_raw, v_s_raw = _mm_ws(h_p, h_s, [(w_a_in[l], d_u)], d_u, _gelu_epilogue, [(F32, F32)],
                                      name="sgu_in_v")
            m_p, m_s, vt_p, vt_s = _sgu_mix(u_p, v_p_raw, u_s, v_s_raw, a_ln_g[l], a_ln_b[l],
                                            w_a_spatial[l], b_a_spatial[l], batch, dec_batch)
            sgu_p.append(vt_p.reshape(batch, CHUNK, d_u))
            sgu_s.append(vt_s.reshape(dec_batch, dec_seq, d_u))
            w_mix = w_a_out[l].astype(BF16)
        else:
            b = l - n_a
            q_p, q_s = _mm_ws(h_p, h_s, [(w_q[b], 0)], width, _identity_epilogue, [(BF16, F32)],
                              name="proj_q")
            m_p = _fox_prompt(q_p, kb_p, vb_p, c_p, batch, n_heads)
            m_s = _fox_sample(q_s, k_s, v_s, lf_s, cache_k, cache_v, ck_pages, page_table, dec_seq)
            w_mix = w_o[b].astype(BF16)
        x_p, x_s, h_p, h_s = _mm_res(m_p, m_s, w_mix, x_p, x_s, g_post[l, 1], g_pre[l, 2:3], 1.0,
                                     bm=256, name="mixer_out")

        if l + 1 < depth:
            g_next = g_pre[l + 1, 0:1]
            if l + 1 == n_a:
                g_next = jnp.concatenate([g_next, g_kv.reshape(1, d_model)], axis=0)
            outs = ffn(x_p, x_s, h_p, h_s, l, 1, g_next)
            x_p, x_s, h_p, h_s = outs[:4]
            if l + 1 == n_a:
                hk_p, hk_s = outs[4:6]
        else:
            x_p, x_s = ffn(x_p, x_s, h_p, h_s, l, 1, None)

    k_p, k_s, v_p, v_s, lf_p, lf_s = kv_out
    return (x_p.reshape(batch, seq, d_model),
            x_s.reshape(dec_batch, dec_seq, d_model),
            k_p.reshape(batch, seq, n_heads, HEAD_DIM),
            v_p.reshape(batch, seq, n_heads, HEAD_DIM),
            lf_p[:, :n_heads].reshape(batch, seq, n_heads),
            k_s.reshape(dec_batch, dec_seq, n_heads, HEAD_DIM),
            v_s.reshape(dec_batch, dec_seq, n_heads, HEAD_DIM),
            lf_s[:, :n_heads].reshape(dec_batch, dec_seq, n_heads),
            jnp.stack(sgu_p),
            jnp.stack(sgu_s))
```

```python
import functools

import jax
import jax.numpy as jnp
from jax import lax
from jax.experimental import pallas as pl
from jax.experimental.pallas import tpu as pltpu

F32 = jnp.float32
BF16 = jnp.bfloat16

RMS_EPS = 1e-6
LN_EPS = 1e-5
NEG_INF = -1e30
LOG2E = 1.4426950408889634
CHUNK = 128
N_GROUPS = 16
HEAD_DIM = 128
PAGE = 128

V7X_VMEM_LIMIT_CAP = 56 * 1024 * 1024
LANES = 128


def _vmem_limit(estimate_bytes):
    return int(min(max(estimate_bytes * 5 // 4, 16 * 1024 * 1024), V7X_VMEM_LIMIT_CAP))


def _params(estimate_bytes, n_grid_dims):
    return pltpu.CompilerParams(
        dimension_semantics=("arbitrary",) * n_grid_dims,
        vmem_limit_bytes=_vmem_limit(estimate_bytes))


def _rms(x, g):
    return x * lax.rsqrt(jnp.mean(x * x, axis=-1, keepdims=True) + RMS_EPS) * g


def _dot(a, b):
    return jnp.dot(a, b, preferred_element_type=F32)


def _dot_exact(a, b):
    return jnp.dot(a, b, preferred_element_type=F32, precision=lax.Precision.HIGHEST)


def _weight_spec(w, lead, block, index):
    n_lead = len(lead)
    assert w.ndim == n_lead + 2
    return pl.BlockSpec((None,) * n_lead + block, lambda *g: tuple(lead) + index(*g))


def _prenorm_body(xp_ref, xs_ref, g_ref, hp_ref, hs_ref, *, n_tiles):
    i = pl.program_id(0)

    @pl.when(i < n_tiles)
    def _():
        hp_ref[...] = _rms(xp_ref[...], g_ref[...]).astype(BF16)

    @pl.when(i == n_tiles)
    def _():
        hs_ref[...] = _rms(xs_ref[...], g_ref[...]).astype(BF16)


def _prenorm(x_p, x_s, g, bm=512):
    mp, d = x_p.shape
    ms = x_s.shape[0]
    n_tiles = mp // bm
    last = n_tiles - 1
    row = lambda i: (jnp.minimum(i, last), 0)
    fixed = lambda i: (0, 0)
    return pl.pallas_call(
        functools.partial(_prenorm_body, n_tiles=n_tiles),
        grid=(n_tiles + 1,),
        in_specs=[pl.BlockSpec((bm, d), row), pl.BlockSpec((ms, d), fixed),
                  pl.BlockSpec((1, d), fixed)],
        out_specs=[pl.BlockSpec((bm, d), row), pl.BlockSpec((ms, d), fixed)],
        out_shape=[jax.ShapeDtypeStruct((mp, d), BF16), jax.ShapeDtypeStruct((ms, d), BF16)],
        compiler_params=_params(6 * bm * d * 4, 1),
        name="prenorm",
    )(x_p, x_s, g.reshape(1, d))


def _mm_ws_body(*refs, n_w, n_out, epilogue):
    hp_ref, hs_ref = refs[0], refs[1]
    w_refs = refs[2:2 + n_w]
    out_refs = refs[2 + n_w:2 + n_w + 2 * n_out]
    wbf_refs = refs[2 + n_w + 2 * n_out:]
    m = pl.program_id(1)

    def run(h_ref, o_refs):
        h = h_ref[...]
        outs = epilogue(*[_dot(h, wbf_ref[...]) for wbf_ref in wbf_refs])
        for o_ref, o in zip(o_refs, outs):
            o_ref[...] = o.astype(o_ref.dtype)

    @pl.when(m == 0)
    def _():
        for w_ref, wbf_ref in zip(w_refs, wbf_refs):
            wbf_ref[...] = w_ref[...].astype(BF16)
        run(hs_ref, out_refs[1::2])

    @pl.when(m > 0)
    def _():
        run(hp_ref, out_refs[0::2])


def _mm_ws(h_p, h_s, weights, n_cols, epilogue, out_dtypes, *, bm=1024, bn=512, name):
    mp, k = h_p.shape
    ms = h_s.shape[0]
    n_tiles = mp // bm
    n_w, n_out = len(weights), len(out_dtypes)
    prow = lambda m: jnp.maximum(m - 1, 0)
    in_specs = [pl.BlockSpec((bm, k), lambda n, m: (prow(m), 0)),
                pl.BlockSpec((ms, k), lambda n, m: (0, 0))]
    for w, lead, off in weights:
        assert off % bn == 0
        in_specs.append(_weight_spec(w, lead, (k, bn), lambda n, m, o=off // bn: (0, n + o)))
    out_specs, out_shape = [], []
    for dt_p, dt_s in out_dtypes:
        out_specs += [pl.BlockSpec((bm, bn), lambda n, m: (prow(m), n)),
                      pl.BlockSpec((ms, bn), lambda n, m: (0, n))]
        out_shape += [jax.ShapeDtypeStruct((mp, n_cols), dt_p),
                      jax.ShapeDtypeStruct((ms, n_cols), dt_s)]
    est = (2 * bm * k * 2 + n_w * k * bn * (2 * 4 + 2)
           + n_out * 2 * bm * bn * 4 + (n_w + n_out) * bm * bn * 4)
    return pl.pallas_call(
        functools.partial(_mm_ws_body, n_w=n_w, n_out=n_out, epilogue=epilogue),
        grid=(n_cols // bn, n_tiles + 1),
        in_specs=in_specs, out_specs=out_specs, out_shape=out_shape,
        scratch_shapes=[pltpu.VMEM((k, bn), BF16) for _ in weights],
        compiler_params=_params(est, 2),
        name=name,
    )(h_p, h_s, *[w for w, _, _ in weights])


def _swiglu_epilogue(g, u):
    return (g / (1.0 + jnp.exp(-g)) * u,)


def _gelu_epilogue(z):
    return (0.5 * z * (1.0 + lax.erf(z * (2.0 ** -0.5))),)


def _identity2_epilogue(z):
    return (z, z)


def _scaled_epilogue(z, *, factor):
    return (z * factor,)


def _mm_res_body(*refs, n_h, n_k, bk, n_tiles, coef):
    ap_ref, as_ref, w_ref, xp_ref, xs_ref, gpost_ref, gnext_ref = refs[:7]
    out_refs = refs[7:7 + 2 * (1 + n_h)]
    wbf_ref = refs[7 + 2 * (1 + n_h)]
    i = pl.program_id(0)

    @pl.when(i < n_k)
    def _():
        r0 = pl.multiple_of(i * bk, bk)
        wbf_ref[pl.ds(r0, bk), :] = w_ref[...].astype(BF16)

    def run(a_ref, x_ref, o_refs):
        y = _dot(a_ref[...].astype(BF16), wbf_ref[...])
        x_new = x_ref[...] + coef * _rms(y, gpost_ref[...])
        o_refs[0][...] = x_new
        if n_h:
            xn = x_new * lax.rsqrt(jnp.mean(x_new * x_new, axis=-1, keepdims=True) + RMS_EPS)
            for j in range(n_h):
                o_refs[1 + j][...] = (xn * gnext_ref[j:j + 1, :]).astype(BF16)

    @pl.when(jnp.logical_and(i >= n_k, i < n_k + n_tiles))
    def _():
        run(ap_ref, xp_ref, out_refs[0::2])

    @pl.when(i == n_k + n_tiles)
    def _():
        run(as_ref, xs_ref, out_refs[1::2])


def _mm_res(a_p, a_s, w, lead, x_p, x_s, g_post, g_next, coef, *, bm, bk, name):
    mp, k = a_p.shape
    ms = a_s.shape[0]
    d = w.shape[-1]
    n_h = 0 if g_next is None else g_next.shape[0]
    g_next_arr = jnp.zeros((1, d), F32) if g_next is None else g_next
    n_tiles = mp // bm
    n_k = k // bk
    assert n_k * bk == k
    row = lambda i: (jnp.clip(i - n_k, 0, n_tiles - 1), 0)
    fixed = lambda i: (0, 0)
    in_specs = [pl.BlockSpec((bm, k), row), pl.BlockSpec((ms, k), fixed),
                _weight_spec(w, lead, (bk, d), lambda i: (jnp.minimum(i, n_k - 1), 0)),
                pl.BlockSpec((bm, d), row), pl.BlockSpec((ms, d), fixed),
                pl.BlockSpec((1, d), fixed), pl.BlockSpec(g_next_arr.shape, fixed)]
    out_specs = [pl.BlockSpec((bm, d), row), pl.BlockSpec((ms, d), fixed)]
    out_shape = [jax.ShapeDtypeStruct((mp, d), F32), jax.ShapeDtypeStruct((ms, d), F32)]
    for _ in range(n_h):
        out_specs += [pl.BlockSpec((bm, d), row), pl.BlockSpec((ms, d), fixed)]
        out_shape += [jax.ShapeDtypeStruct((mp, d), BF16), jax.ShapeDtypeStruct((ms, d), BF16)]
    est = (k * d * 2 + 2 * bk * d * 4 + 2 * bm * k * a_p.dtype.itemsize + 4 * bm * d * 4
           + n_h * 2 * bm * d * 2 + 3 * bm * d * 4)
    return pl.pallas_call(
        functools.partial(_mm_res_body, n_h=n_h, n_k=n_k, bk=bk, n_tiles=n_tiles, coef=coef),
        grid=(n_k + n_tiles + 1,),
        in_specs=in_specs, out_specs=out_specs, out_shape=out_shape,
        scratch_shapes=[pltpu.VMEM((k, d), BF16)],
        compiler_params=_params(est, 1),
        name=name,
    )(a_p, a_s, w, x_p, x_s, g_post.reshape(1, d), g_next_arr)


def _sgu_body(up_ref, vp_ref, us_ref, vs_ref, lng_ref, lnb_ref, wp_ref, bp_ref, ws_ref, bs_ref,
              pp_ref, ps_ref, vtp_ref, vts_ref, *, n_chunks):
    c = pl.program_id(0)

    def run(u_ref, v_ref, w_ref, b_ref, p_ref, vt_ref):
        rows = v_ref.shape[0]
        gw = v_ref.shape[1] // N_GROUPS
        v = v_ref[...]
        xc = v - jnp.mean(v, axis=-1, keepdims=True)
        var = jnp.mean(xc * xc, axis=-1, keepdims=True)
        vn = xc * lax.rsqrt(var + LN_EPS) * lng_ref[...] + lnb_ref[...]
        vt_ref[...] = vn
        vb = vn.astype(BF16)
        causal = (lax.broadcasted_iota(jnp.int32, (rows, rows), 0)
                  >= lax.broadcasted_iota(jnp.int32, (rows, rows), 1))
        for g in range(N_GROUPS):
            wm = jnp.where(causal, w_ref[g], 0.0).astype(BF16)
            mixed = _dot(wm, vb[:, g * gw:(g + 1) * gw]) + b_ref[:, g:g + 1]
            p_ref[:, g * gw:(g + 1) * gw] = (
                u_ref[:, g * gw:(g + 1) * gw].astype(F32) * mixed).astype(BF16)

    @pl.when(c < n_chunks)
    def _():
        run(up_ref, vp_ref, wp_ref, bp_ref, pp_ref, vtp_ref)

    @pl.when(c == n_chunks)
    def _():
        run(us_ref, vs_ref, ws_ref, bs_ref, ps_ref, vts_ref)


def _sgu_mix(u_p, v_p, u_s, v_s, ln_g, ln_b, w_sp, b_sp, batch, dec_batch):
    mp, du = u_p.shape
    ms = u_s.shape[0]
    n_chunks = mp // CHUNK
    chunks_per_seq = n_chunks // batch
    dec_seq = ms // dec_batch
    last = n_chunks - 1
    eye = jnp.eye(dec_batch, dtype=F32)
    w_s = (eye[None, :, None, :, None] * w_sp[:, None, :dec_seq, None, :dec_seq]
           ).reshape(N_GROUPS, ms, ms)
    b_s = jnp.tile(jnp.transpose(b_sp[:, :dec_seq]), (dec_batch, 1))
    b_p = jnp.transpose(b_sp)
    row = lambda c: (jnp.minimum(c, last), 0)
    fixed2 = lambda c: (0, 0)
    fixed3 = lambda c: (0, 0, 0)
    est = 2 * CHUNK * du * (2 + 4 + 2 + 4) + 8 * CHUNK * du * 4
    return pl.pallas_call(
        functools.partial(_sgu_body, n_chunks=n_chunks),
        grid=(n_chunks + 1,),
        in_specs=[pl.BlockSpec((CHUNK, du), row), pl.BlockSpec((CHUNK, du), row),
                  pl.BlockSpec((ms, du), fixed2), pl.BlockSpec((ms, du), fixed2),
                  pl.BlockSpec((1, du), fixed2), pl.BlockSpec((1, du), fixed2),
                  pl.BlockSpec((N_GROUPS, CHUNK, CHUNK), fixed3), pl.BlockSpec((CHUNK, N_GROUPS), fixed2),
                  pl.BlockSpec((N_GROUPS, ms, ms), fixed3), pl.BlockSpec((ms, N_GROUPS), fixed2)],
        out_specs=[pl.BlockSpec((CHUNK, du), row), pl.BlockSpec((ms, du), fixed2),
                   pl.BlockSpec((CHUNK, du), lambda c: (jnp.minimum(c, last) // chunks_per_seq, 0)),
                   pl.BlockSpec((ms, du), fixed2)],
        out_shape=[jax.ShapeDtypeStruct((mp, du), BF16), jax.ShapeDtypeStruct((ms, du), BF16),
                   jax.ShapeDtypeStruct((batch * CHUNK, du), F32), jax.ShapeDtypeStruct((ms, du), F32)],
        compiler_params=_params(est, 1),
        name="sgu_mix",
    )(u_p, v_p, u_s, v_s, ln_g.reshape(1, du), ln_b.reshape(1, du), w_sp, b_p, w_s, b_s)


def _logf_body(xp_ref, xs_ref, g_ref, w_ref, b_ref, lp_ref, ls_ref, *, n_tiles):
    i = pl.program_id(0)

    def run(x_ref, o_ref):
        z = _dot_exact(_rms(x_ref[...], g_ref[...]), w_ref[...]) + b_ref[...]
        o_ref[...] = jnp.minimum(z, 0.0) - jnp.log1p(jnp.exp(-jnp.abs(z)))

    @pl.when(i < n_tiles)
    def _():
        run(xp_ref, lp_ref)

    @pl.when(i == n_tiles)
    def _():
        run(xs_ref, ls_ref)


def _logf(x_p, x_s, g_kv, w_f, b_f, bm=512):
    mp, d = x_p.shape
    ms = x_s.shape[0]
    nh = w_f.shape[1]
    w_pad = jnp.pad(w_f, ((0, 0), (0, LANES - nh)))
    b_pad = jnp.pad(b_f, (0, LANES - nh)).reshape(1, LANES)
    n_tiles = mp // bm
    last = n_tiles - 1
    row = lambda i: (jnp.minimum(i, last), 0)
    fixed = lambda i: (0, 0)
    return pl.pallas_call(
        functools.partial(_logf_body, n_tiles=n_tiles),
        grid=(n_tiles + 1,),
        in_specs=[pl.BlockSpec((bm, d), row), pl.BlockSpec((ms, d), fixed),
                  pl.BlockSpec((1, d), fixed), pl.BlockSpec((d, LANES), fixed),
                  pl.BlockSpec((1, LANES), fixed)],
        out_specs=[pl.BlockSpec((bm, LANES), row), pl.BlockSpec((ms, LANES), fixed)],
        out_shape=[jax.ShapeDtypeStruct((mp, LANES), F32), jax.ShapeDtypeStruct((ms, LANES), F32)],
        compiler_params=_params(8 * bm * d * 4, 1),
        name="logf",
    )(x_p, x_s, g_kv.reshape(1, d), w_pad, b_pad)


def _cumsum_body(l_ref, c_ref, *, n_chunks):
    tril = (lax.broadcasted_iota(jnp.int32, (CHUNK, CHUNK), 0)
            >= lax.broadcasted_iota(jnp.int32, (CHUNK, CHUNK), 1)).astype(F32)

    def step(j, carry):
        r0 = pl.multiple_of(j * CHUNK, CHUNK)
        local = _dot_exact(tril, l_ref[pl.ds(r0, CHUNK), :]) + carry
        c_ref[pl.ds(r0, CHUNK), :] = local
        return local[CHUNK - 1:CHUNK, :]

    lax.fori_loop(0, n_chunks, step, jnp.zeros((1, LANES), F32))


def _cumsum_rows(lf_p, batch):
    mp = lf_p.shape[0]
    seq = mp // batch
    return pl.pallas_call(
        functools.partial(_cumsum_body, n_chunks=seq // CHUNK),
        grid=(batch,),
        in_specs=[pl.BlockSpec((seq, LANES), lambda b: (b, 0))],
        out_specs=pl.BlockSpec((seq, LANES), lambda b: (b, 0)),
        out_shape=jax.ShapeDtypeStruct((mp, LANES), F32),
        compiler_params=_params(4 * seq * LANES * 4, 1),
        name="cumsum_logf",
    )(lf_p)


def _head_column(c, h):
    lane = lax.broadcasted_iota(jnp.int32, (1, LANES), 1)
    return jnp.sum(jnp.where(lane == h, c, 0.0), axis=-1, keepdims=True) * LOG2E


def _split3(c):
    c1 = c.astype(BF16).astype(F32)
    r1 = c - c1
    c2 = r1.astype(BF16).astype(F32)
    c3 = (r1 - c2).astype(BF16).astype(F32)
    return c1, c2, c3


def _bias_columns(c_col, query_side):
    rows = c_col.shape[0]
    lane = lax.broadcasted_iota(jnp.int32, (rows, LANES), 1)
    c1, c2, c3 = _split3(c_col)
    if query_side:
        aug = jnp.where(lane == 0, c1, jnp.where(lane == 1, c2, jnp.where(lane == 2, c3, jnp.where(lane < 6, 1.0, 0.0))))
    else:
        aug = jnp.where(lane < 3, 1.0, jnp.where(lane == 3, -c1, jnp.where(lane == 4, -c2, jnp.where(lane == 5, -c3, 0.0))))
    return aug.astype(BF16)


def _fox_prompt_body(q_ref, k_ref, v_ref, cq_ref, ck_ref, o_ref, kaug_ref, *, tq, tk, hs):
    hg = pl.program_id(1)
    qi = pl.program_id(2)
    head_cols = [slice(a * HEAD_DIM, (a + 1) * HEAD_DIM) for a in range(hs)]
    aug_cols = [slice(a * 2 * HEAD_DIM, (a + 1) * 2 * HEAD_DIM) for a in range(hs)]

    @pl.when(qi == 0)
    def _():
        ck = ck_ref[...]
        for a in range(hs):
            kaug_ref[:, aug_cols[a]] = jnp.concatenate(
                [k_ref[:, head_cols[a]], _bias_columns(_head_column(ck, hg * hs + a), False)], axis=1)

    cq = cq_ref[...]
    q = [jnp.concatenate([q_ref[:, head_cols[a]], _bias_columns(_head_column(cq, hg * hs + a), True)],
                         axis=1) for a in range(hs)]

    def tile(j, carry, masked):
        r0 = pl.multiple_of(j * tk, tk)
        out = []
        for a in range(hs):
            m, l, acc = carry[a]
            s = lax.dot_general(q[a], kaug_ref[pl.ds(r0, tk), aug_cols[a]], (((1,), (1,)), ((), ())),
                                preferred_element_type=F32)
            if masked:
                rows = qi * tq + lax.broadcasted_iota(jnp.int32, (tq, tk), 0)
                cols = j * tk + lax.broadcasted_iota(jnp.int32, (tq, tk), 1)
                s = jnp.where(cols <= rows, s, NEG_INF)
            m_new = jnp.maximum(m, jnp.max(s, axis=-1, keepdims=True))
            alpha = jnp.exp2(m - m_new)
            p = jnp.exp2(s - m_new)
            l = alpha * l + jnp.sum(p, axis=-1, keepdims=True)
            acc = alpha * acc + _dot(p.astype(BF16), v_ref[pl.ds(r0, tk), head_cols[a]])
            out.append((m_new, l, acc))
        return tuple(out)

    init = tuple((jnp.full((tq, 1), NEG_INF, F32), jnp.zeros((tq, 1), F32),
                  jnp.zeros((tq, HEAD_DIM), F32)) for _ in range(hs))
    n_full = (qi * tq) // tk
    carry = lax.fori_loop(0, n_full, lambda j, c: tile(j, c, False), init)
    for d in range(tq // tk):
        carry = tile(n_full + d, carry, True)
    for a in range(hs):
        _, l, acc = carry[a]
        o_ref[:, head_cols[a]] = (acc / l).astype(o_ref.dtype)


def _fox_prompt(q_p, k_bf, v_bf, c_p, batch, tq=512, tk=512, hs=2):
    mp, width = q_p.shape
    n_heads = width // HEAD_DIM
    seq = mp // batch
    nq = seq // tq
    hw = hs * HEAD_DIM
    est = 2 * (2 * seq * hw * 2) + 3 * seq * LANES * 4 + seq * 2 * hw * 2 + hs * 8 * tq * tk * 4
    return pl.pallas_call(
        functools.partial(_fox_prompt_body, tq=tq, tk=tk, hs=hs),
        grid=(batch, n_heads // hs, nq),
        in_specs=[pl.BlockSpec((tq, hw), lambda b, h, i: (b * nq + i, h)),
                  pl.BlockSpec((seq, hw), lambda b, h, i: (b, h)),
                  pl.BlockSpec((seq, hw), lambda b, h, i: (b, h)),
                  pl.BlockSpec((tq, LANES), lambda b, h, i: (b * nq + i, 0)),
                  pl.BlockSpec((seq, LANES), lambda b, h, i: (b, 0))],
        out_specs=pl.BlockSpec((tq, hw), lambda b, h, i: (b * nq + i, h)),
        out_shape=jax.ShapeDtypeStruct((mp, width), BF16),
        scratch_shapes=[pltpu.VMEM((seq, 2 * hw), BF16)],
        compiler_params=_params(est, 3),
        name="fox_prompt",
    )(q_p, k_bf, v_bf, c_p, c_p)


def _ck_past_body(pt_ref, *refs, n_group, n_heads, dec_seq):
    lf_refs = refs[:n_group]
    out_ref, carry_ref = refs[n_group:]
    p = pl.program_id(1)

    @pl.when(p == 0)
    def _():
        carry_ref[...] = jnp.zeros_like(carry_ref)

    expand = (lax.broadcasted_iota(jnp.int32, (n_heads, LANES), 1) // dec_seq
              == lax.broadcasted_iota(jnp.int32, (n_heads, LANES), 0)).astype(F32)
    later = (lax.broadcasted_iota(jnp.int32, (PAGE, PAGE), 1)
             > lax.broadcasted_iota(jnp.int32, (PAGE, PAGE), 0)).astype(F32)
    carry = carry_ref[...]
    for g in range(n_group):
        lf_e = _dot_exact(lf_refs[g][0], expand)
        within = _dot_exact(later, lf_e)
        out_ref[0, n_group - 1 - g] = -(within + carry)
        carry = carry + within[0:1, :] + lf_e[0:1, :]
    carry_ref[...] = carry


def _ck_past(cache_logf, page_table, dec_seq, n_group=8):
    n_seq, n_pages = page_table.shape
    n_heads = cache_logf.shape[2]
    assert n_heads * dec_seq == LANES and n_pages % n_group == 0
    lf_map = lambda b, p, pt, g: (pt[b, n_pages - 1 - (p * n_group + g)], 0, 0)
    n_steps = n_pages // n_group
    grid_spec = pltpu.PrefetchScalarGridSpec(
        num_scalar_prefetch=1,
        grid=(n_seq, n_steps),
        in_specs=[pl.BlockSpec((1, PAGE, n_heads), functools.partial(lf_map, g=g))
                  for g in range(n_group)],
        out_specs=pl.BlockSpec((1, n_group, PAGE, LANES), lambda b, p, pt: (b, n_steps - 1 - p, 0, 0)),
        scratch_shapes=[pltpu.VMEM((1, LANES), F32)])
    return pl.pallas_call(
        functools.partial(_ck_past_body, n_group=n_group, n_heads=n_heads, dec_seq=dec_seq),
        grid_spec=grid_spec,
        out_shape=jax.ShapeDtypeStruct((n_seq, n_pages, PAGE, LANES), F32),
        compiler_params=_params(64 * PAGE * LANES * 4, 2),
        name="ck_past",
    )(page_table, *([cache_logf] * n_group))


def _fox_sample_body(pt_ref, *refs, n_group, n_heads, dec_seq):
    k_refs = refs[:n_group]
    v_refs = refs[n_group:2 * n_group]
    ck_ref, qt_ref, kn_ref, vn_ref, lfn_ref, o_ref, m_ref, l_ref, acc_ref, cq_ref = refs[2 * n_group:]
    p = pl.program_id(1)
    n_steps = pl.num_programs(1)
    qt = qt_ref[0]
    eye = (lax.broadcasted_iota(jnp.int32, (LANES, LANES), 0)
           == lax.broadcasted_iota(jnp.int32, (LANES, LANES), 1))
    head_match = (lax.broadcasted_iota(jnp.int32, (n_heads, LANES), 1) // dec_seq
                  == lax.broadcasted_iota(jnp.int32, (n_heads, LANES), 0))

    def to_column(row):
        return jnp.sum(jnp.where(eye, row, 0.0), axis=-1, keepdims=True)

    def scores(rows_ref, bias, keep):
        s = _dot(rows_ref[0].astype(BF16), qt)
        s = s.reshape(bias.shape[0], n_heads, LANES) + bias[:, None, :]
        return jnp.where(keep, s, NEG_INF)

    def col_max(s3):
        return jnp.max(jnp.max(s3, axis=0), axis=0, keepdims=True)

    def col_sum(p3):
        return jnp.sum(jnp.sum(p3, axis=0), axis=0, keepdims=True)

    def weighted_values(p3, rows_ref):
        pr = p3.reshape(p3.shape[0] * n_heads, LANES).astype(BF16)
        return lax.dot_general(pr, rows_ref[0].astype(BF16), (((0,), (0,)), ((), ())),
                               preferred_element_type=F32)

    @pl.when(p == 0)
    def _():
        expand = (lax.broadcasted_iota(jnp.int32, (LANES, LANES), 1) // dec_seq
                  == lax.broadcasted_iota(jnp.int32, (LANES, LANES), 0)).astype(F32)
        lf_e = _dot_exact(lfn_ref[...], expand)
        rows = [lf_e[0:1, :]]
        for j in range(1, dec_seq):
            rows.append(rows[-1] + lf_e[j:j + 1, :])
        c_new = jnp.concatenate(rows, axis=0)
        key_j = lax.broadcasted_iota(jnp.int32, (dec_seq, LANES), 0)
        col_t = lax.broadcasted_iota(jnp.int32, (dec_seq, LANES), 1) % dec_seq
        cq = jnp.sum(jnp.where(key_j == col_t, c_new, 0.0), axis=0, keepdims=True)
        cq_ref[...] = cq
        causal = (key_j <= col_t)[:, None, :]
        s3 = scores(kn_ref, (cq - c_new) * LOG2E, jnp.logical_and(causal, head_match[None]))
        m = col_max(s3)
        p3 = jnp.exp2(s3 - m[None])
        m_ref[...] = m
        l_ref[...] = col_sum(p3)
        acc_ref[...] = weighted_values(p3, vn_ref)

    cq = cq_ref[...]
    s_pages = [scores(k_refs[g], (cq - ck_ref[0, n_group - 1 - g]) * LOG2E, head_match[None])
               for g in range(n_group)]
    m_old = m_ref[...]
    m_new = m_old
    for s3 in s_pages:
        m_new = jnp.maximum(m_new, col_max(s3))
    alpha = jnp.exp2(m_old - m_new)
    l_new = alpha * l_ref[...]
    pv = None
    for g in range(n_group):
        p3 = jnp.exp2(s_pages[g] - m_new[None])
        l_new = l_new + col_sum(p3)
        t = weighted_values(p3, v_refs[g])
        pv = t if pv is None else pv + t
    m_ref[...] = m_new
    l_ref[...] = l_new
    acc_ref[...] = to_column(alpha) * acc_ref[...] + pv

    @pl.when(p == n_steps - 1)
    def _():
        out = acc_ref[...] / to_column(l_ref[...])
        for h in range(n_heads):
            o_ref[:, h * HEAD_DIM:(h + 1) * HEAD_DIM] = out[h * dec_seq:(h + 1) * dec_seq, :]


def _fox_sample(q_s, k_s, v_s, lf_s, cache_k, cache_v, ck_pages, page_table, dec_seq, n_group=4):
    n_seq, n_pages = page_table.shape
    n_heads = cache_k.shape[2]
    width = n_heads * HEAD_DIM
    page_rows = PAGE * n_heads
    new_rows = dec_seq * n_heads
    ck_rows = cache_k.reshape(cache_k.shape[0], page_rows, HEAD_DIM)
    cv_rows = cache_v.reshape(cache_v.shape[0], page_rows, HEAD_DIM)
    kn_rows = k_s.reshape(n_seq, new_rows, HEAD_DIM)
    vn_rows = v_s.reshape(n_seq, new_rows, HEAD_DIM)
    qt = jnp.transpose(q_s.reshape(n_seq, dec_seq, n_heads, HEAD_DIM), (0, 3, 2, 1))
    qt = qt.reshape(n_seq, HEAD_DIM, n_heads * dec_seq).astype(BF16)
    page_map = lambda b, p, pt, g: (pt[b, n_pages - 1 - (p * n_group + g)], 0, 0)
    n_steps = n_pages // n_group
    seq3 = lambda b, p, pt: (b, 0, 0)
    page_block = (1, page_rows, HEAD_DIM)
    in_specs = ([pl.BlockSpec(page_block, functools.partial(page_map, g=g)) for g in range(n_group)]
                + [pl.BlockSpec(page_block, functools.partial(page_map, g=g)) for g in range(n_group)]
                + [pl.BlockSpec((1, n_group, PAGE, LANES), lambda b, p, pt: (b, n_steps - 1 - p, 0, 0)),
                   pl.BlockSpec((1, HEAD_DIM, LANES), seq3),
                   pl.BlockSpec((1, new_rows, HEAD_DIM), seq3), pl.BlockSpec((1, new_rows, HEAD_DIM), seq3),
                   pl.BlockSpec((dec_seq, LANES), lambda b, p, pt: (b, 0))])
    grid_spec = pltpu.PrefetchScalarGridSpec(
        num_scalar_prefetch=1,
        grid=(n_seq, n_steps),
        in_specs=in_specs,
        out_specs=pl.BlockSpec((dec_seq, width), lambda b, p, pt: (b, 0)),
        scratch_shapes=[pltpu.VMEM((1, LANES), F32), pltpu.VMEM((1, LANES), F32),
                        pltpu.VMEM((LANES, HEAD_DIM), F32), pltpu.VMEM((1, LANES), F32)])
    est = 2 * 2 * n_group * page_rows * HEAD_DIM * 4 + n_group * page_rows * LANES * 12
    return pl.pallas_call(
        functools.partial(_fox_sample_body, n_group=n_group, n_heads=n_heads, dec_seq=dec_seq),
        grid_spec=grid_spec,
        out_shape=jax.ShapeDtypeStruct((n_seq * dec_seq, width), F32),
        compiler_params=_params(est, 2),
        name="fox_sample",
    )(page_table, *([ck_rows] * n_group), *([cv_rows] * n_group), ck_pages, qt, kn_rows, vn_rows, lf_s)


def kernel(x_prompt, x_sample, cache_k, cache_v, cache_logf, page_table, g_pre, g_post, w_ffn_gate, w_ffn_up, w_ffn_down, w_a_in, a_ln_g, a_ln_b, w_a_spatial, b_a_spatial, w_a_out, g_kv, w_kvf, b_f, w_q, w_o):
    batch, seq, d_model = x_prompt.shape
    dec_batch, dec_seq, _ = x_sample.shape
    depth = g_pre.shape[0]
    n_a = w_a_in.shape[0]
    d_ff = w_ffn_gate.shape[-1]
    d_u = a_ln_g.shape[-1]
    n_heads = b_f.shape[0]
    width = n_heads * HEAD_DIM
    q_factor = LOG2E * HEAD_DIM ** -0.5

    x_p = x_prompt.reshape(batch * seq, d_model)
    x_s = x_sample.reshape(dec_batch * dec_seq, d_model)
    h_p, h_s = _prenorm(x_p, x_s, g_pre[0, 0])

    def ffn(x_p, x_s, h_p, h_s, l, j, g_next):
        a_p, a_s = _mm_ws(h_p, h_s, [(w_ffn_gate, (l, j), 0), (w_ffn_up, (l, j), 0)], d_ff,
                          _swiglu_epilogue, [(BF16, BF16)], name="ffn_gate_up")
        return _mm_res(a_p, a_s, w_ffn_down, (l, j), x_p, x_s, g_post[l, 2 * j], g_next,
                       0.5, bm=256, bk=512, name="ffn_down")

    sgu_p, sgu_s = [], []
    kv_out = None
    for l in range(depth):
        if l == n_a:
            k_p, k_s, kb_p, _ = _mm_ws(hk_p, hk_s, [(w_kvf, (), 0)], width, _identity2_epilogue,
                                       [(F32, F32), (BF16, BF16)], name="proj_k")
            v_p, v_s, vb_p, _ = _mm_ws(hk_p, hk_s, [(w_kvf, (), width)], width, _identity2_epilogue,
                                       [(F32, F32), (BF16, BF16)], name="proj_v")
            lf_p, lf_s = _logf(x_p, x_s, g_kv, w_kvf[:, 2 * width:], b_f)
            c_p = _cumsum_rows(lf_p, batch)
            ck_pages = _ck_past(cache_logf, page_table, dec_seq)
            kv_out = (k_p, k_s, v_p, v_s, lf_p, lf_s)

        x_p, x_s, h_p, h_s = ffn(x_p, x_s, h_p, h_s, l, 0, g_pre[l, 1:2])

        if l < n_a:
            u_p, u_s = _mm_ws(h_p, h_s, [(w_a_in, (l,), 0)], d_u, _gelu_epilogue, [(BF16, BF16)],
                              name="sgu_in_u")
            v_p_raw, v_s_raw = _mm_ws(h_p, h_s, [(w_a_in, (l,), d_u)], d_u, _gelu_epilogue, [(F32, F32)],
                                      name="sgu_in_v")
            m_p, m_s, vt_p, vt_s = _sgu_mix(u_p, v_p_raw, u_s, v_s_raw, a_ln_g[l], a_ln_b[l],
                                            w_a_spatial[l], b_a_spatial[l], batch, dec_batch)
            sgu_p.append(vt_p.reshape(batch, CHUNK, d_u))
            sgu_s.append(vt_s.reshape(dec_batch, dec_seq, d_u))
            w_mix, lead, bk = w_a_out, (l,), 512
        else:
            b = l - n_a
            q_p, q_s = _mm_ws(h_p, h_s, [(w_q, (b,), 0)], width,
                              functools.partial(_scaled_epilogue, factor=q_factor), [(BF16, F32)],
                              name="proj_q")
            m_p = _fox_prompt(q_p, kb_p, vb_p, c_p, batch)
            m_s = _fox_sample(q_s, k_s, v_s, lf_s, cache_k, cache_v, ck_pages, page_table, dec_seq)
            w_mix, lead, bk = w_o, (b,), 512
        x_p, x_s, h_p, h_s = _mm_res(m_p, m_s, w_mix, lead, x_p, x_s, g_post[l, 1], g_pre[l, 2:3], 1.0,
                                     bm=256, bk=bk, name="mixer_out")

        if l + 1 < depth:
            g_next = g_pre[l + 1, 0:1]
            if l + 1 == n_a:
                g_next = jnp.concatenate([g_next, g_kv.reshape(1, d_model)], axis=0)
            outs = ffn(x_p, x_s, h_p, h_s, l, 1, g_next)
            x_p, x_s, h_p, h_s = outs[:4]
            if l + 1 == n_a:
                hk_p, hk_s = outs[4:6]
        else:
            x_p, x_s = ffn(x_p, x_s, h_p, h_s, l, 1, None)

    k_p, k_s, v_p, v_s, lf_p, lf_s = kv_out
    return (x_p.reshape(batch, seq, d_model),
            x_s.reshape(dec_batch, dec_seq, d_model),
            k_p.reshape(batch, seq, n_heads, HEAD_DIM),
            v_p.reshape(batch, seq, n_heads, HEAD_DIM),
            lf_p[:, :n_heads].reshape(batch, seq, n_heads),
            k_s.reshape(dec_batch, dec_seq, n_heads, HEAD_DIM),
            v_s.reshape(dec_batch, dec_seq, n_heads, HEAD_DIM),
            lf_s[:, :n_heads].reshape(dec_batch, dec_seq, n_heads),
            jnp.stack(sgu_p),
            jnp.stack(sgu_s))
```

```python
import functools

import jax
import jax.numpy as jnp
from jax import lax
from jax.experimental import pallas as pl
from jax.experimental.pallas import tpu as pltpu

F32 = jnp.float32
BF16 = jnp.bfloat16

RMS_EPS = 1e-6
LN_EPS = 1e-5
NEG_INF = -1e30
LOG2E = 1.4426950408889634
CHUNK = 128
N_GROUPS = 16
HEAD_DIM = 128
PAGE = 128

V7X_VMEM_LIMIT_CAP = 56 * 1024 * 1024
LANES = 128


def _vmem_limit(estimate_bytes):
    return int(min(max(estimate_bytes * 5 // 4, 16 * 1024 * 1024), V7X_VMEM_LIMIT_CAP))


def _params(estimate_bytes, n_grid_dims):
    return pltpu.CompilerParams(
        dimension_semantics=("arbitrary",) * n_grid_dims,
        vmem_limit_bytes=_vmem_limit(estimate_bytes))


def _rms(x, g):
    return x * lax.rsqrt(jnp.mean(x * x, axis=-1, keepdims=True) + RMS_EPS) * g


def _dot(a, b):
    return jnp.dot(a, b, preferred_element_type=F32)


def _dot_exact(a, b):
    return jnp.dot(a, b, preferred_element_type=F32, precision=lax.Precision.HIGHEST)


def _weight_spec(w, lead, block, index):
    n_lead = len(lead)
    assert w.ndim == n_lead + 2
    return pl.BlockSpec((None,) * n_lead + block, lambda *g: tuple(lead) + index(*g))


def _prenorm_body(xp_ref, xs_ref, g_ref, hp_ref, hs_ref, *, n_tiles):
    i = pl.program_id(0)

    @pl.when(i < n_tiles)
    def _():
        hp_ref[...] = _rms(xp_ref[...], g_ref[...]).astype(BF16)

    @pl.when(i == n_tiles)
    def _():
        hs_ref[...] = _rms(xs_ref[...], g_ref[...]).astype(BF16)


def _prenorm(x_p, x_s, g, bm=512):
    mp, d = x_p.shape
    ms = x_s.shape[0]
    n_tiles = mp // bm
    last = n_tiles - 1
    row = lambda i: (jnp.minimum(i, last), 0)
    fixed = lambda i: (0, 0)
    return pl.pallas_call(
        functools.partial(_prenorm_body, n_tiles=n_tiles),
        grid=(n_tiles + 1,),
        in_specs=[pl.BlockSpec((bm, d), row), pl.BlockSpec((ms, d), fixed),
                  pl.BlockSpec((1, d), fixed)],
        out_specs=[pl.BlockSpec((bm, d), row), pl.BlockSpec((ms, d), fixed)],
        out_shape=[jax.ShapeDtypeStruct((mp, d), BF16), jax.ShapeDtypeStruct((ms, d), BF16)],
        compiler_params=_params(6 * bm * d * 4, 1),
        name="prenorm",
    )(x_p, x_s, g.reshape(1, d))


def _mm_ws_body(*refs, n_w, n_out, epilogue):
    hp_ref, hs_ref = refs[0], refs[1]
    w_refs = refs[2:2 + n_w]
    out_refs = refs[2 + n_w:2 + n_w + 2 * n_out]
    wbf_refs = refs[2 + n_w + 2 * n_out:]
    m = pl.program_id(1)

    def run(h_ref, o_refs):
        h = h_ref[...]
        outs = epilogue(*[_dot(h, wbf_ref[...]) for wbf_ref in wbf_refs])
        for o_ref, o in zip(o_refs, outs):
            o_ref[...] = o.astype(o_ref.dtype)

    @pl.when(m == 0)
    def _():
        for w_ref, wbf_ref in zip(w_refs, wbf_refs):
            wbf_ref[...] = w_ref[...].astype(BF16)
        run(hs_ref, out_refs[1::2])

    @pl.when(m > 0)
    def _():
        run(hp_ref, out_refs[0::2])


def _mm_ws(h_p, h_s, weights, n_cols, epilogue, out_dtypes, *, bm=1024, bn=512, name):
    mp, k = h_p.shape
    ms = h_s.shape[0]
    n_tiles = mp // bm
    n_w, n_out = len(weights), len(out_dtypes)
    prow = lambda m: jnp.maximum(m - 1, 0)
    in_specs = [pl.BlockSpec((bm, k), lambda n, m: (prow(m), 0)),
                pl.BlockSpec((ms, k), lambda n, m: (0, 0))]
    for w, lead, off in weights:
        assert off % bn == 0
        in_specs.append(_weight_spec(w, lead, (k, bn), lambda n, m, o=off // bn: (0, n + o)))
    out_specs, out_shape = [], []
    for dt_p, dt_s in out_dtypes:
        out_specs += [pl.BlockSpec((bm, bn), lambda n, m: (prow(m), n)),
                      pl.BlockSpec((ms, bn), lambda n, m: (0, n))]
        out_shape += [jax.ShapeDtypeStruct((mp, n_cols), dt_p),
                      jax.ShapeDtypeStruct((ms, n_cols), dt_s)]
    est = (2 * bm * k * 2 + n_w * k * bn * (2 * 4 + 2)
           + n_out * 2 * bm * bn * 4 + (n_w + n_out) * bm * bn * 4)
    return pl.pallas_call(
        functools.partial(_mm_ws_body, n_w=n_w, n_out=n_out, epilogue=epilogue),
        grid=(n_cols // bn, n_tiles + 1),
        in_specs=in_specs, out_specs=out_specs, out_shape=out_shape,
        scratch_shapes=[pltpu.VMEM((k, bn), BF16) for _ in weights],
        compiler_params=_params(est, 2),
        name=name,
    )(h_p, h_s, *[w for w, _, _ in weights])


def _swiglu_epilogue(g, u):
    return (g / (1.0 + jnp.exp(-g)) * u,)


def _gelu(z):
    return 0.5 * z * (1.0 + lax.erf(z * (2.0 ** -0.5)))


def _gelu2_epilogue(zu, zv):
    return (_gelu(zu), _gelu(zv))


def _kv_epilogue(zk, zv):
    return (zk, zk, zv, zv)


def _scaled_epilogue(z, *, factor):
    return (z * factor,)


def _mm_res_body(*refs, n_h, n_k, bk, n_tiles, coef):
    ap_ref, as_ref, w_ref, xp_ref, xs_ref, gpost_ref, gnext_ref = refs[:7]
    out_refs = refs[7:7 + 2 * (1 + n_h)]
    wbf_ref = refs[7 + 2 * (1 + n_h)]
    i = pl.program_id(0)

    @pl.when(i < n_k)
    def _():
        r0 = pl.multiple_of(i * bk, bk)
        wbf_ref[pl.ds(r0, bk), :] = w_ref[...].astype(BF16)

    def run(a_ref, x_ref, o_refs):
        y = _dot(a_ref[...].astype(BF16), wbf_ref[...])
        x_new = x_ref[...] + coef * _rms(y, gpost_ref[...])
        o_refs[0][...] = x_new
        if n_h:
            xn = x_new * lax.rsqrt(jnp.mean(x_new * x_new, axis=-1, keepdims=True) + RMS_EPS)
            for j in range(n_h):
                o_refs[1 + j][...] = (xn * gnext_ref[j:j + 1, :]).astype(BF16)

    @pl.when(jnp.logical_and(i >= n_k, i < n_k + n_tiles))
    def _():
        run(ap_ref, xp_ref, out_refs[0::2])

    @pl.when(i == n_k + n_tiles)
    def _():
        run(as_ref, xs_ref, out_refs[1::2])


def _mm_res(a_p, a_s, w, lead, x_p, x_s, g_post, g_next, coef, *, bm, bk, name):
    mp, k = a_p.shape
    ms = a_s.shape[0]
    d = w.shape[-1]
    n_h = 0 if g_next is None else g_next.shape[0]
    g_next_arr = jnp.zeros((1, d), F32) if g_next is None else g_next
    n_tiles = mp // bm
    n_k = k // bk
    assert n_k * bk == k
    row = lambda i: (jnp.clip(i - n_k, 0, n_tiles - 1), 0)
    fixed = lambda i: (0, 0)
    in_specs = [pl.BlockSpec((bm, k), row), pl.BlockSpec((ms, k), fixed),
                _weight_spec(w, lead, (bk, d), lambda i: (jnp.minimum(i, n_k - 1), 0)),
                pl.BlockSpec((bm, d), row), pl.BlockSpec((ms, d), fixed),
                pl.BlockSpec((1, d), fixed), pl.BlockSpec(g_next_arr.shape, fixed)]
    out_specs = [pl.BlockSpec((bm, d), row), pl.BlockSpec((ms, d), fixed)]
    out_shape = [jax.ShapeDtypeStruct((mp, d), F32), jax.ShapeDtypeStruct((ms, d), F32)]
    for _ in range(n_h):
        out_specs += [pl.BlockSpec((bm, d), row), pl.BlockSpec((ms, d), fixed)]
        out_shape += [jax.ShapeDtypeStruct((mp, d), BF16), jax.ShapeDtypeStruct((ms, d), BF16)]
    est = (k * d * 2 + 2 * bk * d * 4 + 2 * bm * k * a_p.dtype.itemsize + 4 * bm * d * 4
           + n_h * 2 * bm * d * 2 + 3 * bm * d * 4)
    return pl.pallas_call(
        functools.partial(_mm_res_body, n_h=n_h, n_k=n_k, bk=bk, n_tiles=n_tiles, coef=coef),
        grid=(n_k + n_tiles + 1,),
        in_specs=in_specs, out_specs=out_specs, out_shape=out_shape,
        scratch_shapes=[pltpu.VMEM((k, d), BF16)],
        compiler_params=_params(est, 1),
        name=name,
    )(a_p, a_s, w, x_p, x_s, g_post.reshape(1, d), g_next_arr)


def _sgu_body(up_ref, vp_ref, us_ref, vs_ref, lng_ref, lnb_ref, wp_ref, bp_ref, ws_ref, bs_ref,
              pp_ref, ps_ref, vtp_ref, vts_ref, *, n_chunks):
    c = pl.program_id(0)

    def run(u_ref, v_ref, w_ref, b_ref, p_ref, vt_ref):
        rows = v_ref.shape[0]
        gw = v_ref.shape[1] // N_GROUPS
        v = v_ref[...]
        xc = v - jnp.mean(v, axis=-1, keepdims=True)
        var = jnp.mean(xc * xc, axis=-1, keepdims=True)
        vn = xc * lax.rsqrt(var + LN_EPS) * lng_ref[...] + lnb_ref[...]
        vt_ref[...] = vn
        vb = vn.astype(BF16)
        causal = (lax.broadcasted_iota(jnp.int32, (rows, rows), 0)
                  >= lax.broadcasted_iota(jnp.int32, (rows, rows), 1))
        for g in range(N_GROUPS):
            wm = jnp.where(causal, w_ref[g], 0.0).astype(BF16)
            mixed = _dot(wm, vb[:, g * gw:(g + 1) * gw]) + b_ref[:, g:g + 1]
            p_ref[:, g * gw:(g + 1) * gw] = (
                u_ref[:, g * gw:(g + 1) * gw].astype(F32) * mixed).astype(BF16)

    @pl.when(c < n_chunks)
    def _():
        run(up_ref, vp_ref, wp_ref, bp_ref, pp_ref, vtp_ref)

    @pl.when(c == n_chunks)
    def _():
        run(us_ref, vs_ref, ws_ref, bs_ref, ps_ref, vts_ref)


def _sgu_mix(u_p, v_p, u_s, v_s, ln_g, ln_b, w_sp, b_sp, batch, dec_batch):
    mp, du = u_p.shape
    ms = u_s.shape[0]
    n_chunks = mp // CHUNK
    chunks_per_seq = n_chunks // batch
    dec_seq = ms // dec_batch
    last = n_chunks - 1
    eye = jnp.eye(dec_batch, dtype=F32)
    w_s = (eye[None, :, None, :, None] * w_sp[:, None, :dec_seq, None, :dec_seq]
           ).reshape(N_GROUPS, ms, ms)
    b_s = jnp.tile(jnp.transpose(b_sp[:, :dec_seq]), (dec_batch, 1))
    b_p = jnp.transpose(b_sp)
    row = lambda c: (jnp.minimum(c, last), 0)
    fixed2 = lambda c: (0, 0)
    fixed3 = lambda c: (0, 0, 0)
    est = 2 * CHUNK * du * (2 + 4 + 2 + 4) + 8 * CHUNK * du * 4
    return pl.pallas_call(
        functools.partial(_sgu_body, n_chunks=n_chunks),
        grid=(n_chunks + 1,),
        in_specs=[pl.BlockSpec((CHUNK, du), row), pl.BlockSpec((CHUNK, du), row),
                  pl.BlockSpec((ms, du), fixed2), pl.BlockSpec((ms, du), fixed2),
                  pl.BlockSpec((1, du), fixed2), pl.BlockSpec((1, du), fixed2),
                  pl.BlockSpec((N_GROUPS, CHUNK, CHUNK), fixed3), pl.BlockSpec((CHUNK, N_GROUPS), fixed2),
                  pl.BlockSpec((N_GROUPS, ms, ms), fixed3), pl.BlockSpec((ms, N_GROUPS), fixed2)],
        out_specs=[pl.BlockSpec((CHUNK, du), row), pl.BlockSpec((ms, du), fixed2),
                   pl.BlockSpec((CHUNK, du), lambda c: (jnp.minimum(c, last) // chunks_per_seq, 0)),
                   pl.BlockSpec((ms, du), fixed2)],
        out_shape=[jax.ShapeDtypeStruct((mp, du), BF16), jax.ShapeDtypeStruct((ms, du), BF16),
                   jax.ShapeDtypeStruct((batch * CHUNK, du), F32), jax.ShapeDtypeStruct((ms, du), F32)],
        compiler_params=_params(est, 1),
        name="sgu_mix",
    )(u_p, v_p, u_s, v_s, ln_g.reshape(1, du), ln_b.reshape(1, du), w_sp, b_p, w_s, b_s)


def _logf_body(xp_ref, xs_ref, g_ref, w_ref, b_ref, lp_ref, ls_ref, *, n_tiles):
    i = pl.program_id(0)

    def run(x_ref, o_ref):
        z = _dot_exact(_rms(x_ref[...], g_ref[...]), w_ref[...]) + b_ref[...]
        o_ref[...] = jnp.minimum(z, 0.0) - jnp.log1p(jnp.exp(-jnp.abs(z)))

    @pl.when(i < n_tiles)
    def _():
        run(xp_ref, lp_ref)

    @pl.when(i == n_tiles)
    def _():
        run(xs_ref, ls_ref)


def _logf(x_p, x_s, g_kv, w_f, b_f, bm=512):
    mp, d = x_p.shape
    ms = x_s.shape[0]
    nh = w_f.shape[1]
    w_pad = jnp.pad(w_f, ((0, 0), (0, LANES - nh)))
    b_pad = jnp.pad(b_f, (0, LANES - nh)).reshape(1, LANES)
    n_tiles = mp // bm
    last = n_tiles - 1
    row = lambda i: (jnp.minimum(i, last), 0)
    fixed = lambda i: (0, 0)
    return pl.pallas_call(
        functools.partial(_logf_body, n_tiles=n_tiles),
        grid=(n_tiles + 1,),
        in_specs=[pl.BlockSpec((bm, d), row), pl.BlockSpec((ms, d), fixed),
                  pl.BlockSpec((1, d), fixed), pl.BlockSpec((d, LANES), fixed),
                  pl.BlockSpec((1, LANES), fixed)],
        out_specs=[pl.BlockSpec((bm, LANES), row), pl.BlockSpec((ms, LANES), fixed)],
        out_shape=[jax.ShapeDtypeStruct((mp, LANES), F32), jax.ShapeDtypeStruct((ms, LANES), F32)],
        compiler_params=_params(8 * bm * d * 4, 1),
        name="logf",
    )(x_p, x_s, g_kv.reshape(1, d), w_pad, b_pad)


def _cumsum_body(l_ref, c_ref, *, n_chunks):
    tril = (lax.broadcasted_iota(jnp.int32, (CHUNK, CHUNK), 0)
            >= lax.broadcasted_iota(jnp.int32, (CHUNK, CHUNK), 1)).astype(F32)

    def step(j, carry):
        r0 = pl.multiple_of(j * CHUNK, CHUNK)
        local = _dot_exact(tril, l_ref[pl.ds(r0, CHUNK), :]) + carry
        c_ref[pl.ds(r0, CHUNK), :] = local
        return local[CHUNK - 1:CHUNK, :]

    lax.fori_loop(0, n_chunks, step, jnp.zeros((1, LANES), F32))


def _cumsum_rows(lf_p, batch):
    mp = lf_p.shape[0]
    seq = mp // batch
    return pl.pallas_call(
        functools.partial(_cumsum_body, n_chunks=seq // CHUNK),
        grid=(batch,),
        in_specs=[pl.BlockSpec((seq, LANES), lambda b: (b, 0))],
        out_specs=pl.BlockSpec((seq, LANES), lambda b: (b, 0)),
        out_shape=jax.ShapeDtypeStruct((mp, LANES), F32),
        compiler_params=_params(4 * seq * LANES * 4, 1),
        name="cumsum_logf",
    )(lf_p)


def _head_column(c, h):
    lane = lax.broadcasted_iota(jnp.int32, (1, LANES), 1)
    return jnp.sum(jnp.where(lane == h, c, 0.0), axis=-1, keepdims=True) * LOG2E


def _split3(c):
    c1 = c.astype(BF16).astype(F32)
    r1 = c - c1
    c2 = r1.astype(BF16).astype(F32)
    c3 = (r1 - c2).astype(BF16).astype(F32)
    return c1, c2, c3


def _bias_columns(c_col, query_side):
    rows = c_col.shape[0]
    lane = lax.broadcasted_iota(jnp.int32, (rows, LANES), 1)
    c1, c2, c3 = _split3(c_col)
    if query_side:
        aug = jnp.where(lane == 0, c1, jnp.where(lane == 1, c2, jnp.where(lane == 2, c3, jnp.where(lane < 6, 1.0, 0.0))))
    else:
        aug = jnp.where(lane < 3, 1.0, jnp.where(lane == 3, -c1, jnp.where(lane == 4, -c2, jnp.where(lane == 5, -c3, 0.0))))
    return aug.astype(BF16)


def _fox_prompt_body(q_ref, k_ref, v_ref, cq_ref, ck_ref, o_ref, kaug_ref, vt_ref, *, tq, tk, hs, nq):
    hg = pl.program_id(1)
    qi = pl.program_id(2)
    head_cols = [slice(a * HEAD_DIM, (a + 1) * HEAD_DIM) for a in range(hs)]
    aug_cols = [slice(a * 2 * HEAD_DIM, (a + 1) * 2 * HEAD_DIM) for a in range(hs)]

    @pl.when(qi == 0)
    def _():
        ck = ck_ref[...]
        for a in range(hs):
            kaug_ref[:, aug_cols[a]] = jnp.concatenate(
                [k_ref[:, head_cols[a]], _bias_columns(_head_column(ck, hg * hs + a), False)], axis=1)
            vt_ref[a] = v_ref[:, head_cols[a]].astype(F32).T.astype(BF16)

    cq = cq_ref[...]
    q_t = [jnp.concatenate([q_ref[:, head_cols[a]].astype(F32),
                            _bias_columns(_head_column(cq, hg * hs + a), True).astype(F32)],
                           axis=1).T.astype(BF16) for a in range(hs)]

    def tile(j, carry, masked):
        r0 = j * tk
        out = []
        for a in range(hs):
            m, l, acc = carry[a]
            s = _dot(kaug_ref[pl.ds(r0, tk), aug_cols[a]], q_t[a])
            if masked:
                s = jnp.where(lax.broadcasted_iota(jnp.int32, (tk, tq), 0)
                              <= lax.broadcasted_iota(jnp.int32, (tk, tq), 1), s, NEG_INF)
            m_new = jnp.maximum(m, jnp.max(s, axis=0, keepdims=True))
            alpha = jnp.exp2(m - m_new)
            p = jnp.exp2(s - m_new)
            l = alpha * l + jnp.sum(p, axis=0, keepdims=True)
            acc = alpha * acc + _dot(vt_ref[a, :, pl.ds(r0, tk)], p.astype(BF16))
            out.append((m_new, l, acc))
        return tuple(out)

    for n_full in range(nq):
        @pl.when(qi == n_full)
        def _(n_full=n_full):
            carry = tuple((jnp.full((1, tq), NEG_INF, F32), jnp.zeros((1, tq), F32),
                           jnp.zeros((HEAD_DIM, tq), F32)) for _ in range(hs))
            for j in range(n_full):
                carry = tile(j, carry, False)
            carry = tile(n_full, carry, True)
            for a in range(hs):
                _, l, acc = carry[a]
                o_ref[:, head_cols[a]] = (acc / l).T.astype(o_ref.dtype)


def _fox_prompt(q_p, k_bf, v_bf, c_p, batch, tq=1024, tk=1024, hs=1):
    assert tq == tk
    mp, width = q_p.shape
    n_heads = width // HEAD_DIM
    seq = mp // batch
    nq = seq // tq
    hw = hs * HEAD_DIM
    est = 2 * (2 * seq * hw * 2) + 3 * seq * LANES * 4 + 3 * seq * hw * 2 + hs * 6 * tq * tk * 4
    return pl.pallas_call(
        functools.partial(_fox_prompt_body, tq=tq, tk=tk, hs=hs, nq=nq),
        grid=(batch, n_heads // hs, nq),
        in_specs=[pl.BlockSpec((tq, hw), lambda b, h, i: (b * nq + i, h)),
                  pl.BlockSpec((seq, hw), lambda b, h, i: (b, h)),
                  pl.BlockSpec((seq, hw), lambda b, h, i: (b, h)),
                  pl.BlockSpec((tq, LANES), lambda b, h, i: (b * nq + i, 0)),
                  pl.BlockSpec((seq, LANES), lambda b, h, i: (b, 0))],
        out_specs=pl.BlockSpec((tq, hw), lambda b, h, i: (b * nq + i, h)),
        out_shape=jax.ShapeDtypeStruct((mp, width), BF16),
        scratch_shapes=[pltpu.VMEM((seq, 2 * hw), BF16), pltpu.VMEM((hs, HEAD_DIM, seq), BF16)],
        compiler_params=_params(est, 3),
        name="fox_prompt",
    )(q_p, k_bf, v_bf, c_p, c_p)


def _ck_past_body(pt_ref, *refs, n_group, n_heads, dec_seq):
    lf_refs = refs[:n_group]
    out_ref, carry_ref = refs[n_group:]
    p = pl.program_id(1)

    @pl.when(p == 0)
    def _():
        carry_ref[...] = jnp.zeros_like(carry_ref)

    expand = (lax.broadcasted_iota(jnp.int32, (n_heads, LANES), 1) // dec_seq
              == lax.broadcasted_iota(jnp.int32, (n_heads, LANES), 0)).astype(BF16)
    later = (lax.broadcasted_iota(jnp.int32, (PAGE, PAGE), 1)
             > lax.broadcasted_iota(jnp.int32, (PAGE, PAGE), 0)).astype(BF16)
    lf = jnp.concatenate([lf_refs[g][0] for g in range(n_group)], axis=0)
    lf_e, within = None, None
    for term in _split3(lf):
        e = _dot(term.astype(BF16), expand)
        e_by_lanes = jnp.concatenate([e[g * PAGE:(g + 1) * PAGE] for g in range(n_group)], axis=1)
        w = _dot(later, e_by_lanes.astype(BF16))
        lf_e = e if lf_e is None else lf_e + e
        within = w if within is None else within + w
    carry = carry_ref[...]
    for g in range(n_group):
        within_g = within[:, g * LANES:(g + 1) * LANES]
        out_ref[0, n_group - 1 - g] = -(within_g + carry)
        carry = carry + within_g[0:1, :] + lf_e[g * PAGE:g * PAGE + 1, :]
    carry_ref[...] = carry


def _ck_past(cache_logf, page_table, dec_seq, n_group=8):
    n_seq, n_pages = page_table.shape
    n_heads = cache_logf.shape[2]
    assert n_heads * dec_seq == LANES and n_pages % n_group == 0
    lf_map = lambda b, p, pt, g: (pt[b, n_pages - 1 - (p * n_group + g)], 0, 0)
    n_steps = n_pages // n_group
    grid_spec = pltpu.PrefetchScalarGridSpec(
        num_scalar_prefetch=1,
        grid=(n_seq, n_steps),
        in_specs=[pl.BlockSpec((1, PAGE, n_heads), functools.partial(lf_map, g=g))
                  for g in range(n_group)],
        out_specs=pl.BlockSpec((1, n_group, PAGE, LANES), lambda b, p, pt: (b, n_steps - 1 - p, 0, 0)),
        scratch_shapes=[pltpu.VMEM((1, LANES), F32)])
    return pl.pallas_call(
        functools.partial(_ck_past_body, n_group=n_group, n_heads=n_heads, dec_seq=dec_seq),
        grid_spec=grid_spec,
        out_shape=jax.ShapeDtypeStruct((n_seq, n_pages, PAGE, LANES), F32),
        compiler_params=_params(64 * PAGE * LANES * 4, 2),
        name="ck_past",
    )(page_table, *([cache_logf] * n_group))


def _fox_sample_body(pt_ref, *refs, n_group, n_heads, dec_seq):
    k_refs = refs[:n_group]
    v_refs = refs[n_group:2 * n_group]
    ck_ref, qt_ref, kn_ref, vn_ref, lfn_ref, o_ref, m_ref, l_ref, acc_ref, cq_ref = refs[2 * n_group:]
    p = pl.program_id(1)
    n_steps = pl.num_programs(1)
    qt = qt_ref[0]
    eye = (lax.broadcasted_iota(jnp.int32, (LANES, LANES), 0)
           == lax.broadcasted_iota(jnp.int32, (LANES, LANES), 1))
    head_match = (lax.broadcasted_iota(jnp.int32, (n_heads, LANES), 1) // dec_seq
                  == lax.broadcasted_iota(jnp.int32, (n_heads, LANES), 0))

    def to_column(row):
        return jnp.sum(jnp.where(eye, row, 0.0), axis=-1, keepdims=True)

    def scores(rows_ref, bias, keep):
        s = _dot(rows_ref[0].astype(BF16), qt)
        s = s.reshape(bias.shape[0], n_heads, LANES) + bias[:, None, :]
        return jnp.where(keep, s, NEG_INF)

    def col_max(s3):
        return jnp.max(jnp.max(s3, axis=0), axis=0, keepdims=True)

    def col_sum(p3):
        return jnp.sum(jnp.sum(p3, axis=0), axis=0, keepdims=True)

    def weighted_values(p3, rows_ref):
        pr = p3.reshape(p3.shape[0] * n_heads, LANES).astype(BF16)
        return lax.dot_general(pr, rows_ref[0].astype(BF16), (((0,), (0,)), ((), ())),
                               preferred_element_type=F32)

    @pl.when(p == 0)
    def _():
        expand = (lax.broadcasted_iota(jnp.int32, (LANES, LANES), 1) // dec_seq
                  == lax.broadcasted_iota(jnp.int32, (LANES, LANES), 0)).astype(F32)
        lf_e = _dot_exact(lfn_ref[...], expand)
        rows = [lf_e[0:1, :]]
        for j in range(1, dec_seq):
            rows.append(rows[-1] + lf_e[j:j + 1, :])
        c_new = jnp.concatenate(rows, axis=0)
        key_j = lax.broadcasted_iota(jnp.int32, (dec_seq, LANES), 0)
        col_t = lax.broadcasted_iota(jnp.int32, (dec_seq, LANES), 1) % dec_seq
        cq = jnp.sum(jnp.where(key_j == col_t, c_new, 0.0), axis=0, keepdims=True)
        cq_ref[...] = cq
        causal = (key_j <= col_t)[:, None, :]
        s3 = scores(kn_ref, (cq - c_new) * LOG2E, jnp.logical_and(causal, head_match[None]))
        m = col_max(s3)
        p3 = jnp.exp2(s3 - m[None])
        m_ref[...] = m
        l_ref[...] = col_sum(p3)
        acc_ref[...] = weighted_values(p3, vn_ref)

    cq = cq_ref[...]
    s_pages = [scores(k_refs[g], (cq - ck_ref[0, n_group - 1 - g]) * LOG2E, head_match[None])
               for g in range(n_group)]
    m_old = m_ref[...]
    m_new = m_old
    for s3 in s_pages:
        m_new = jnp.maximum(m_new, col_max(s3))
    alpha = jnp.exp2(m_old - m_new)
    l_new = alpha * l_ref[...]
    pv = None
    for g in range(n_group):
        p3 = jnp.exp2(s_pages[g] - m_new[None])
        l_new = l_new + col_sum(p3)
        t = weighted_values(p3, v_refs[g])
        pv = t if pv is None else pv + t
    m_ref[...] = m_new
    l_ref[...] = l_new
    acc_ref[...] = to_column(alpha) * acc_ref[...] + pv

    @pl.when(p == n_steps - 1)
    def _():
        out = acc_ref[...] / to_column(l_ref[...])
        for h in range(n_heads):
            o_ref[:, h * HEAD_DIM:(h + 1) * HEAD_DIM] = out[h * dec_seq:(h + 1) * dec_seq, :]


def _fox_sample(q_s, k_s, v_s, lf_s, cache_k, cache_v, ck_pages, page_table, dec_seq, n_group=4):
    n_seq, n_pages = page_table.shape
    n_heads = cache_k.shape[2]
    width = n_heads * HEAD_DIM
    page_rows = PAGE * n_heads
    new_rows = dec_seq * n_heads
    ck_rows = cache_k.reshape(cache_k.shape[0], page_rows, HEAD_DIM)
    cv_rows = cache_v.reshape(cache_v.shape[0], page_rows, HEAD_DIM)
    kn_rows = k_s.reshape(n_seq, new_rows, HEAD_DIM)
    vn_rows = v_s.reshape(n_seq, new_rows, HEAD_DIM)
    qt = jnp.transpose(q_s.reshape(n_seq, dec_seq, n_heads, HEAD_DIM), (0, 3, 2, 1))
    qt = qt.reshape(n_seq, HEAD_DIM, n_heads * dec_seq).astype(BF16)
    page_map = lambda b, p, pt, g: (pt[b, n_pages - 1 - (p * n_group + g)], 0, 0)
    n_steps = n_pages // n_group
    seq3 = lambda b, p, pt: (b, 0, 0)
    page_block = (1, page_rows, HEAD_DIM)
    in_specs = ([pl.BlockSpec(page_block, functools.partial(page_map, g=g)) for g in range(n_group)]
                + [pl.BlockSpec(page_block, functools.partial(page_map, g=g)) for g in range(n_group)]
                + [pl.BlockSpec((1, n_group, PAGE, LANES), lambda b, p, pt: (b, n_steps - 1 - p, 0, 0)),
                   pl.BlockSpec((1, HEAD_DIM, LANES), seq3),
                   pl.BlockSpec((1, new_rows, HEAD_DIM), seq3), pl.BlockSpec((1, new_rows, HEAD_DIM), seq3),
                   pl.BlockSpec((dec_seq, LANES), lambda b, p, pt: (b, 0))])
    grid_spec = pltpu.PrefetchScalarGridSpec(
        num_scalar_prefetch=1,
        grid=(n_seq, n_steps),
        in_specs=in_specs,
        out_specs=pl.BlockSpec((dec_seq, width), lambda b, p, pt: (b, 0)),
        scratch_shapes=[pltpu.VMEM((1, LANES), F32), pltpu.VMEM((1, LANES), F32),
                        pltpu.VMEM((LANES, HEAD_DIM), F32), pltpu.VMEM((1, LANES), F32)])
    est = 2 * 2 * n_group * page_rows * HEAD_DIM * 4 + n_group * page_rows * LANES * 12
    return pl.pallas_call(
        functools.partial(_fox_sample_body, n_group=n_group, n_heads=n_heads, dec_seq=dec_seq),
        grid_spec=grid_spec,
        out_shape=jax.ShapeDtypeStruct((n_seq * dec_seq, width), F32),
        compiler_params=_params(est, 2),
        name="fox_sample",
    )(page_table, *([ck_rows] * n_group), *([cv_rows] * n_group), ck_pages, qt, kn_rows, vn_rows, lf_s)


def kernel(x_prompt, x_sample, cache_k, cache_v, cache_logf, page_table, g_pre, g_post, w_ffn_gate, w_ffn_up, w_ffn_down, w_a_in, a_ln_g, a_ln_b, w_a_spatial, b_a_spatial, w_a_out, g_kv, w_kvf, b_f, w_q, w_o):
    batch, seq, d_model = x_prompt.shape
    dec_batch, dec_seq, _ = x_sample.shape
    depth = g_pre.shape[0]
    n_a = w_a_in.shape[0]
    d_ff = w_ffn_gate.shape[-1]
    d_u = a_ln_g.shape[-1]
    n_heads = b_f.shape[0]
    width = n_heads * HEAD_DIM
    q_factor = LOG2E * HEAD_DIM ** -0.5

    x_p = x_prompt.reshape(batch * seq, d_model)
    x_s = x_sample.reshape(dec_batch * dec_seq, d_model)
    h_p, h_s = _prenorm(x_p, x_s, g_pre[0, 0])

    def ffn(x_p, x_s, h_p, h_s, l, j, g_next):
        a_p, a_s = _mm_ws(h_p, h_s, [(w_ffn_gate, (l, j), 0), (w_ffn_up, (l, j), 0)], d_ff,
                          _swiglu_epilogue, [(BF16, BF16)], name="ffn_gate_up")
        return _mm_res(a_p, a_s, w_ffn_down, (l, j), x_p, x_s, g_post[l, 2 * j], g_next,
                       0.5, bm=256, bk=512, name="ffn_down")

    sgu_p, sgu_s = [], []
    kv_out = None
    for l in range(depth):
        if l == n_a:
            k_p, k_s, kb_p, _, v_p, v_s, vb_p, _ = _mm_ws(
                hk_p, hk_s, [(w_kvf, (), 0), (w_kvf, (), width)], width, _kv_epilogue,
                [(F32, F32), (BF16, BF16), (F32, F32), (BF16, BF16)], name="proj_kv")
            lf_p, lf_s = _logf(x_p, x_s, g_kv, w_kvf[:, 2 * width:], b_f)
            c_p = _cumsum_rows(lf_p, batch)
            ck_pages = _ck_past(cache_logf, page_table, dec_seq)
            kv_out = (k_p, k_s, v_p, v_s, lf_p, lf_s)

        x_p, x_s, h_p, h_s = ffn(x_p, x_s, h_p, h_s, l, 0, g_pre[l, 1:2])

        if l < n_a:
            u_p, u_s, v_p_raw, v_s_raw = _mm_ws(
                h_p, h_s, [(w_a_in, (l,), 0), (w_a_in, (l,), d_u)], d_u, _gelu2_epilogue,
                [(BF16, BF16), (F32, F32)], name="sgu_in")
            m_p, m_s, vt_p, vt_s = _sgu_mix(u_p, v_p_raw, u_s, v_s_raw, a_ln_g[l], a_ln_b[l],
                                            w_a_spatial[l], b_a_spatial[l], batch, dec_batch)
            sgu_p.append(vt_p.reshape(batch, CHUNK, d_u))
            sgu_s.append(vt_s.reshape(dec_batch, dec_seq, d_u))
            w_mix, lead, bk = w_a_out, (l,), 512
        else:
            b = l - n_a
            q_p, q_s = _mm_ws(h_p, h_s, [(w_q, (b,), 0)], width,
                              functools.partial(_scaled_epilogue, factor=q_factor), [(BF16, F32)],
                              name="proj_q")
            m_p = _fox_prompt(q_p, kb_p, vb_p, c_p, batch)
            m_s = _fox_sample(q_s, k_s, v_s, lf_s, cache_k, cache_v, ck_pages, page_table, dec_seq)
            w_mix, lead, bk = w_o, (b,), 512
        x_p, x_s, h_p, h_s = _mm_res(m_p, m_s, w_mix, lead, x_p, x_s, g_post[l, 1], g_pre[l, 2:3], 1.0,
                                     bm=256 if l < n_a else 512, bk=bk, name="mixer_out")

        if l + 1 < depth:
            g_next = g_pre[l + 1, 0:1]
            if l + 1 == n_a:
                g_next = jnp.concatenate([g_next, g_kv.reshape(1, d_model)], axis=0)
            outs = ffn(x_p, x_s, h_p, h_s, l, 1, g_next)
            x_p, x_s, h_p, h_s = outs[:4]
            if l + 1 == n_a:
                hk_p, hk_s = outs[4:6]
        else:
            x_p, x_s = ffn(x_p, x_s, h_p, h_s, l, 1, None)

    k_p, k_s, v_p, v_s, lf_p, lf_s = kv_out
    return (x_p.reshape(batch, seq, d_model),
            x_s.reshape(dec_batch, dec_seq, d_model),
            k_p.reshape(batch, seq, n_heads, HEAD_DIM),
            v_p.reshape(batch, seq, n_heads, HEAD_DIM),
            lf_p[:, :n_heads].reshape(batch, seq, n_heads),
            k_s.reshape(dec_batch, dec_seq, n_heads, HEAD_DIM),
            v_s.reshape(dec_batch, dec_seq, n_heads, HEAD_DIM),
            lf_s[:, :n_heads].reshape(dec_batch, dec_seq, n_heads),
            jnp.stack(sgu_p),
            jnp.stack(sgu_s))
```

```python
import functools

import jax
import jax.numpy as jnp
from jax import lax
from jax.experimental import pallas as pl
from jax.experimental.pallas import tpu as pltpu

F32 = jnp.float32
BF16 = jnp.bfloat16

RMS_EPS = 1e-6
LN_EPS = 1e-5
NEG_INF = -1e30
LOG2E = 1.4426950408889634
CHUNK = 128
N_GROUPS = 16
HEAD_DIM = 128
PAGE = 128

V7X_VMEM_LIMIT_CAP = 56 * 1024 * 1024
LANES = 128


def _vmem_limit(estimate_bytes):
    return int(min(max(estimate_bytes * 5 // 4, 16 * 1024 * 1024), V7X_VMEM_LIMIT_CAP))


def _params(estimate_bytes, n_grid_dims):
    return pltpu.CompilerParams(
        dimension_semantics=("arbitrary",) * n_grid_dims,
        vmem_limit_bytes=_vmem_limit(estimate_bytes))


def _rms(x, g):
    return x * lax.rsqrt(jnp.mean(x * x, axis=-1, keepdims=True) + RMS_EPS) * g


def _dot(a, b):
    return jnp.dot(a, b, preferred_element_type=F32)


def _dot_exact(a, b):
    return jnp.dot(a, b, preferred_element_type=F32, precision=lax.Precision.HIGHEST)


def _weight_spec(w, lead, block, index):
    n_lead = len(lead)
    assert w.ndim == n_lead + 2
    return pl.BlockSpec((None,) * n_lead + block, lambda *g: tuple(lead) + index(*g))


def _prenorm_body(xp_ref, xs_ref, g_ref, hp_ref, hs_ref, *, n_tiles):
    i = pl.program_id(0)

    @pl.when(i < n_tiles)
    def _():
        hp_ref[...] = _rms(xp_ref[...], g_ref[...]).astype(BF16)

    @pl.when(i == n_tiles)
    def _():
        hs_ref[...] = _rms(xs_ref[...], g_ref[...]).astype(BF16)


def _prenorm(x_p, x_s, g, bm=512):
    mp, d = x_p.shape
    ms = x_s.shape[0]
    n_tiles = mp // bm
    last = n_tiles - 1
    row = lambda i: (jnp.minimum(i, last), 0)
    fixed = lambda i: (0, 0)
    return pl.pallas_call(
        functools.partial(_prenorm_body, n_tiles=n_tiles),
        grid=(n_tiles + 1,),
        in_specs=[pl.BlockSpec((bm, d), row), pl.BlockSpec((ms, d), fixed),
                  pl.BlockSpec((1, d), fixed)],
        out_specs=[pl.BlockSpec((bm, d), row), pl.BlockSpec((ms, d), fixed)],
        out_shape=[jax.ShapeDtypeStruct((mp, d), BF16), jax.ShapeDtypeStruct((ms, d), BF16)],
        compiler_params=_params(6 * bm * d * 4, 1),
        name="prenorm",
    )(x_p, x_s, g.reshape(1, d))


def _mm_ws_body(*refs, n_w, n_out, epilogue):
    hp_ref, hs_ref = refs[0], refs[1]
    w_refs = refs[2:2 + n_w]
    out_refs = refs[2 + n_w:2 + n_w + 2 * n_out]
    wbf_refs = refs[2 + n_w + 2 * n_out:]
    m = pl.program_id(1)

    def run(h_ref, o_refs):
        h = h_ref[...]
        outs = epilogue(*[_dot(h, wbf_ref[...]) for wbf_ref in wbf_refs])
        for o_ref, o in zip(o_refs, outs):
            o_ref[...] = o.astype(o_ref.dtype)

    @pl.when(m == 0)
    def _():
        for w_ref, wbf_ref in zip(w_refs, wbf_refs):
            wbf_ref[...] = w_ref[...].astype(BF16)
        run(hs_ref, out_refs[1::2])

    @pl.when(m > 0)
    def _():
        run(hp_ref, out_refs[0::2])


def _mm_ws(h_p, h_s, weights, n_cols, epilogue, out_dtypes, *, bm=1024, bn=512, name):
    mp, k = h_p.shape
    ms = h_s.shape[0]
    n_tiles = mp // bm
    n_w, n_out = len(weights), len(out_dtypes)
    prow = lambda m: jnp.maximum(m - 1, 0)
    in_specs = [pl.BlockSpec((bm, k), lambda n, m: (prow(m), 0)),
                pl.BlockSpec((ms, k), lambda n, m: (0, 0))]
    for w, lead, off in weights:
        assert off % bn == 0
        in_specs.append(_weight_spec(w, lead, (k, bn), lambda n, m, o=off // bn: (0, n + o)))
    out_specs, out_shape = [], []
    for dt_p, dt_s in out_dtypes:
        out_specs += [pl.BlockSpec((bm, bn), lambda n, m: (prow(m), n)),
                      pl.BlockSpec((ms, bn), lambda n, m: (0, n))]
        out_shape += [jax.ShapeDtypeStruct((mp, n_cols), dt_p),
                      jax.ShapeDtypeStruct((ms, n_cols), dt_s)]
    est = (2 * bm * k * 2 + n_w * k * bn * (2 * 4 + 2)
           + n_out * 2 * bm * bn * 4 + (n_w + n_out) * bm * bn * 4)
    return pl.pallas_call(
        functools.partial(_mm_ws_body, n_w=n_w, n_out=n_out, epilogue=epilogue),
        grid=(n_cols // bn, n_tiles + 1),
        in_specs=in_specs, out_specs=out_specs, out_shape=out_shape,
        scratch_shapes=[pltpu.VMEM((k, bn), BF16) for _ in weights],
        compiler_params=_params(est, 2),
        name=name,
    )(h_p, h_s, *[w for w, _, _ in weights])


def _swiglu_epilogue(g, u):
    return (g / (1.0 + jnp.exp(-g)) * u,)


def _gelu(z):
    return 0.5 * z * (1.0 + lax.erf(z * (2.0 ** -0.5)))


def _gelu2_epilogue(zu, zv):
    return (_gelu(zu), _gelu(zv))


def _kv_epilogue(zk, zv):
    return (zk, zk, zv, zv)


def _scaled_epilogue(z, *, factor):
    return (z * factor,)


def _mm_res_body(*refs, n_h, n_k, bk, n_tiles, coef):
    ap_ref, as_ref, w_ref, xp_ref, xs_ref, gpost_ref, gnext_ref = refs[:7]
    out_refs = refs[7:7 + 2 * (1 + n_h)]
    wbf_ref, y0_ref = refs[7 + 2 * (1 + n_h):]
    i = pl.program_id(0)

    @pl.when(i < n_k)
    def _():
        r0 = pl.multiple_of(i * bk, bk)
        w_chunk = w_ref[...].astype(BF16)
        wbf_ref[pl.ds(r0, bk), :] = w_chunk
        part = _dot(ap_ref[:, pl.ds(r0, bk)].astype(BF16), w_chunk)

        @pl.when(i == 0)
        def _():
            y0_ref[...] = part

        @pl.when(i > 0)
        def _():
            y0_ref[...] += part

    def run(a_ref, x_ref, o_refs, y=None):
        if y is None:
            y = _dot(a_ref[...].astype(BF16), wbf_ref[...])
        x_new = x_ref[...] + coef * _rms(y, gpost_ref[...])
        o_refs[0][...] = x_new
        if n_h:
            xn = x_new * lax.rsqrt(jnp.mean(x_new * x_new, axis=-1, keepdims=True) + RMS_EPS)
            for j in range(n_h):
                o_refs[1 + j][...] = (xn * gnext_ref[j:j + 1, :]).astype(BF16)

    @pl.when(i == n_k)
    def _():
        run(ap_ref, xp_ref, out_refs[0::2], y0_ref[...])

    @pl.when(jnp.logical_and(i > n_k, i < n_k + n_tiles))
    def _():
        run(ap_ref, xp_ref, out_refs[0::2])

    @pl.when(i == n_k + n_tiles)
    def _():
        run(as_ref, xs_ref, out_refs[1::2])


def _mm_res(a_p, a_s, w, lead, x_p, x_s, g_post, g_next, coef, *, bm, bk, name):
    mp, k = a_p.shape
    ms = a_s.shape[0]
    d = w.shape[-1]
    n_h = 0 if g_next is None else g_next.shape[0]
    g_next_arr = jnp.zeros((1, d), F32) if g_next is None else g_next
    n_tiles = mp // bm
    n_k = k // bk
    assert n_k * bk == k
    row = lambda i: (jnp.clip(i - n_k, 0, n_tiles - 1), 0)
    fixed = lambda i: (0, 0)
    in_specs = [pl.BlockSpec((bm, k), row), pl.BlockSpec((ms, k), fixed),
                _weight_spec(w, lead, (bk, d), lambda i: (jnp.minimum(i, n_k - 1), 0)),
                pl.BlockSpec((bm, d), row), pl.BlockSpec((ms, d), fixed),
                pl.BlockSpec((1, d), fixed), pl.BlockSpec(g_next_arr.shape, fixed)]
    out_specs = [pl.BlockSpec((bm, d), row), pl.BlockSpec((ms, d), fixed)]
    out_shape = [jax.ShapeDtypeStruct((mp, d), F32), jax.ShapeDtypeStruct((ms, d), F32)]
    for _ in range(n_h):
        out_specs += [pl.BlockSpec((bm, d), row), pl.BlockSpec((ms, d), fixed)]
        out_shape += [jax.ShapeDtypeStruct((mp, d), BF16), jax.ShapeDtypeStruct((ms, d), BF16)]
    est = (k * d * 2 + 2 * bk * d * 4 + 2 * bm * k * a_p.dtype.itemsize + 4 * bm * d * 4
           + n_h * 2 * bm * d * 2 + 3 * bm * d * 4)
    return pl.pallas_call(
        functools.partial(_mm_res_body, n_h=n_h, n_k=n_k, bk=bk, n_tiles=n_tiles, coef=coef),
        grid=(n_k + n_tiles + 1,),
        in_specs=in_specs, out_specs=out_specs, out_shape=out_shape,
        scratch_shapes=[pltpu.VMEM((k, d), BF16), pltpu.VMEM((bm, d), F32)],
        compiler_params=_params(est, 1),
        name=name,
    )(a_p, a_s, w, x_p, x_s, g_post.reshape(1, d), g_next_arr)


def _sgu_body(up_ref, vp_ref, us_ref, vs_ref, lng_ref, lnb_ref, wp_ref, bp_ref, ws_ref, bs_ref,
              pp_ref, ps_ref, vtp_ref, vts_ref, *, n_chunks):
    c = pl.program_id(0)

    def run(u_ref, v_ref, w_ref, b_ref, p_ref, vt_ref):
        rows = v_ref.shape[0]
        gw = v_ref.shape[1] // N_GROUPS
        v = v_ref[...]
        xc = v - jnp.mean(v, axis=-1, keepdims=True)
        var = jnp.mean(xc * xc, axis=-1, keepdims=True)
        vn = xc * lax.rsqrt(var + LN_EPS) * lng_ref[...] + lnb_ref[...]
        vt_ref[...] = vn
        vb = vn.astype(BF16)
        causal = (lax.broadcasted_iota(jnp.int32, (rows, rows), 0)
                  >= lax.broadcasted_iota(jnp.int32, (rows, rows), 1))
        for g in range(N_GROUPS):
            wm = jnp.where(causal, w_ref[g], 0.0).astype(BF16)
            mixed = _dot(wm, vb[:, g * gw:(g + 1) * gw]) + b_ref[:, g:g + 1]
            p_ref[:, g * gw:(g + 1) * gw] = (
                u_ref[:, g * gw:(g + 1) * gw].astype(F32) * mixed).astype(BF16)

    @pl.when(c < n_chunks)
    def _():
        run(up_ref, vp_ref, wp_ref, bp_ref, pp_ref, vtp_ref)

    @pl.when(c == n_chunks)
    def _():
        run(us_ref, vs_ref, ws_ref, bs_ref, ps_ref, vts_ref)


def _sgu_mix(u_p, v_p, u_s, v_s, ln_g, ln_b, w_sp, b_sp, batch, dec_batch):
    mp, du = u_p.shape
    ms = u_s.shape[0]
    n_chunks = mp // CHUNK
    chunks_per_seq = n_chunks // batch
    dec_seq = ms // dec_batch
    last = n_chunks - 1
    eye = jnp.eye(dec_batch, dtype=F32)
    w_s = (eye[None, :, None, :, None] * w_sp[:, None, :dec_seq, None, :dec_seq]
           ).reshape(N_GROUPS, ms, ms)
    b_s = jnp.tile(jnp.transpose(b_sp[:, :dec_seq]), (dec_batch, 1))
    b_p = jnp.transpose(b_sp)
    row = lambda c: (jnp.minimum(c, last), 0)
    fixed2 = lambda c: (0, 0)
    fixed3 = lambda c: (0, 0, 0)
    est = 2 * CHUNK * du * (2 + 4 + 2 + 4) + 8 * CHUNK * du * 4
    return pl.pallas_call(
        functools.partial(_sgu_body, n_chunks=n_chunks),
        grid=(n_chunks + 1,),
        in_specs=[pl.BlockSpec((CHUNK, du), row), pl.BlockSpec((CHUNK, du), row),
                  pl.BlockSpec((ms, du), fixed2), pl.BlockSpec((ms, du), fixed2),
                  pl.BlockSpec((1, du), fixed2), pl.BlockSpec((1, du), fixed2),
                  pl.BlockSpec((N_GROUPS, CHUNK, CHUNK), fixed3), pl.BlockSpec((CHUNK, N_GROUPS), fixed2),
                  pl.BlockSpec((N_GROUPS, ms, ms), fixed3), pl.BlockSpec((ms, N_GROUPS), fixed2)],
        out_specs=[pl.BlockSpec((CHUNK, du), row), pl.BlockSpec((ms, du), fixed2),
                   pl.BlockSpec((CHUNK, du), lambda c: (jnp.minimum(c, last) // chunks_per_seq, 0)),
                   pl.BlockSpec((ms, du), fixed2)],
        out_shape=[jax.ShapeDtypeStruct((mp, du), BF16), jax.ShapeDtypeStruct((ms, du), BF16),
                   jax.ShapeDtypeStruct((batch * CHUNK, du), F32), jax.ShapeDtypeStruct((ms, du), F32)],
        compiler_params=_params(est, 1),
        name="sgu_mix",
    )(u_p, v_p, u_s, v_s, ln_g.reshape(1, du), ln_b.reshape(1, du), w_sp, b_p, w_s, b_s)


def _logf_body(xp_ref, xs_ref, g_ref, w_ref, b_ref, lp_ref, ls_ref, *, n_tiles):
    i = pl.program_id(0)

    def run(x_ref, o_ref):
        z = _dot_exact(_rms(x_ref[...], g_ref[...]), w_ref[...]) + b_ref[...]
        o_ref[...] = jnp.minimum(z, 0.0) - jnp.log1p(jnp.exp(-jnp.abs(z)))

    @pl.when(i < n_tiles)
    def _():
        run(xp_ref, lp_ref)

    @pl.when(i == n_tiles)
    def _():
        run(xs_ref, ls_ref)


def _logf(x_p, x_s, g_kv, w_f, b_f, bm=512):
    mp, d = x_p.shape
    ms = x_s.shape[0]
    nh = w_f.shape[1]
    w_pad = jnp.pad(w_f, ((0, 0), (0, LANES - nh)))
    b_pad = jnp.pad(b_f, (0, LANES - nh)).reshape(1, LANES)
    n_tiles = mp // bm
    last = n_tiles - 1
    row = lambda i: (jnp.minimum(i, last), 0)
    fixed = lambda i: (0, 0)
    return pl.pallas_call(
        functools.partial(_logf_body, n_tiles=n_tiles),
        grid=(n_tiles + 1,),
        in_specs=[pl.BlockSpec((bm, d), row), pl.BlockSpec((ms, d), fixed),
                  pl.BlockSpec((1, d), fixed), pl.BlockSpec((d, LANES), fixed),
                  pl.BlockSpec((1, LANES), fixed)],
        out_specs=[pl.BlockSpec((bm, LANES), row), pl.BlockSpec((ms, LANES), fixed)],
        out_shape=[jax.ShapeDtypeStruct((mp, LANES), F32), jax.ShapeDtypeStruct((ms, LANES), F32)],
        compiler_params=_params(8 * bm * d * 4, 1),
        name="logf",
    )(x_p, x_s, g_kv.reshape(1, d), w_pad, b_pad)


def _cumsum_body(l_ref, c_ref, *, n_chunks):
    tril = (lax.broadcasted_iota(jnp.int32, (CHUNK, CHUNK), 0)
            >= lax.broadcasted_iota(jnp.int32, (CHUNK, CHUNK), 1)).astype(F32)

    def step(j, carry):
        r0 = pl.multiple_of(j * CHUNK, CHUNK)
        local = _dot_exact(tril, l_ref[pl.ds(r0, CHUNK), :]) + carry
        c_ref[pl.ds(r0, CHUNK), :] = local
        return local[CHUNK - 1:CHUNK, :]

    lax.fori_loop(0, n_chunks, step, jnp.zeros((1, LANES), F32))


def _cumsum_rows(lf_p, batch):
    mp = lf_p.shape[0]
    seq = mp // batch
    return pl.pallas_call(
        functools.partial(_cumsum_body, n_chunks=seq // CHUNK),
        grid=(batch,),
        in_specs=[pl.BlockSpec((seq, LANES), lambda b: (b, 0))],
        out_specs=pl.BlockSpec((seq, LANES), lambda b: (b, 0)),
        out_shape=jax.ShapeDtypeStruct((mp, LANES), F32),
        compiler_params=_params(4 * seq * LANES * 4, 1),
        name="cumsum_logf",
    )(lf_p)


def _head_column(c, h):
    lane = lax.broadcasted_iota(jnp.int32, (1, LANES), 1)
    return jnp.sum(jnp.where(lane == h, c, 0.0), axis=-1, keepdims=True) * LOG2E


def _split3(c):
    c1 = c.astype(BF16).astype(F32)
    r1 = c - c1
    c2 = r1.astype(BF16).astype(F32)
    c3 = (r1 - c2).astype(BF16).astype(F32)
    return c1, c2, c3


def _bias_columns(c_col, query_side):
    rows = c_col.shape[0]
    lane = lax.broadcasted_iota(jnp.int32, (rows, LANES), 1)
    c1, c2, c3 = _split3(c_col)
    if query_side:
        aug = jnp.where(lane == 0, c1, jnp.where(lane == 1, c2, jnp.where(lane == 2, c3, jnp.where(lane < 6, 1.0, 0.0))))
    else:
        aug = jnp.where(lane < 3, 1.0, jnp.where(lane == 3, -c1, jnp.where(lane == 4, -c2, jnp.where(lane == 5, -c3, 0.0))))
    return aug.astype(BF16)


def _fox_prompt_body(q_ref, k_ref, v_ref, cq_ref, ck_ref, o_ref, kaug_ref, vt_ref, *, tq, tk, hs, nq):
    hg = pl.program_id(1)
    qi = pl.program_id(2)
    head_cols = [slice(a * HEAD_DIM, (a + 1) * HEAD_DIM) for a in range(hs)]
    aug_cols = [slice(a * 2 * HEAD_DIM, (a + 1) * 2 * HEAD_DIM) for a in range(hs)]

    @pl.when(qi == 0)
    def _():
        ck = ck_ref[...]
        for a in range(hs):
            kaug_ref[:, aug_cols[a]] = jnp.concatenate(
                [k_ref[:, head_cols[a]], _bias_columns(_head_column(ck, hg * hs + a), False)], axis=1)
            vt_ref[a] = v_ref[:, head_cols[a]].astype(F32).T.astype(BF16)

    cq = cq_ref[...]
    q_t = [jnp.concatenate([q_ref[:, head_cols[a]].astype(F32),
                            _bias_columns(_head_column(cq, hg * hs + a), True).astype(F32)],
                           axis=1).T.astype(BF16) for a in range(hs)]

    def tile(j, carry, masked):
        r0 = j * tk
        out = []
        for a in range(hs):
            m, l, acc = carry[a]
            s = _dot(kaug_ref[pl.ds(r0, tk), aug_cols[a]], q_t[a])
            if masked:
                s = jnp.where(lax.broadcasted_iota(jnp.int32, (tk, tq), 0)
                              <= lax.broadcasted_iota(jnp.int32, (tk, tq), 1), s, NEG_INF)
            m_new = jnp.maximum(m, jnp.max(s, axis=0, keepdims=True))
            alpha = jnp.exp2(m - m_new)
            p = jnp.exp2(s - m_new)
            l = alpha * l + jnp.sum(p, axis=0, keepdims=True)
            acc = alpha * acc + _dot(vt_ref[a, :, pl.ds(r0, tk)], p.astype(BF16))
            out.append((m_new, l, acc))
        return tuple(out)

    for n_full in range(nq):
        @pl.when(qi == n_full)
        def _(n_full=n_full):
            carry = tuple((jnp.full((1, tq), NEG_INF, F32), jnp.zeros((1, tq), F32),
                           jnp.zeros((HEAD_DIM, tq), F32)) for _ in range(hs))
            for j in range(n_full):
                carry = tile(j, carry, False)
            carry = tile(n_full, carry, True)
            for a in range(hs):
                _, l, acc = carry[a]
                o_ref[:, head_cols[a]] = (acc / l).T.astype(o_ref.dtype)


def _fox_prompt(q_p, k_bf, v_bf, c_p, batch, tq=1024, tk=1024, hs=2):
    assert tq == tk
    mp, width = q_p.shape
    n_heads = width // HEAD_DIM
    seq = mp // batch
    nq = seq // tq
    hw = hs * HEAD_DIM
    est = 2 * (2 * seq * hw * 2) + 3 * seq * LANES * 4 + 3 * seq * hw * 2 + hs * 6 * tq * tk * 4
    return pl.pallas_call(
        functools.partial(_fox_prompt_body, tq=tq, tk=tk, hs=hs, nq=nq),
        grid=(batch, n_heads // hs, nq),
        in_specs=[pl.BlockSpec((tq, hw), lambda b, h, i: (b * nq + i, h)),
                  pl.BlockSpec((seq, hw), lambda b, h, i: (b, h)),
                  pl.BlockSpec((seq, hw), lambda b, h, i: (b, h)),
                  pl.BlockSpec((tq, LANES), lambda b, h, i: (b * nq + i, 0)),
                  pl.BlockSpec((seq, LANES), lambda b, h, i: (b, 0))],
        out_specs=pl.BlockSpec((tq, hw), lambda b, h, i: (b * nq + i, h)),
        out_shape=jax.ShapeDtypeStruct((mp, width), BF16),
        scratch_shapes=[pltpu.VMEM((seq, 2 * hw), BF16), pltpu.VMEM((hs, HEAD_DIM, seq), BF16)],
        compiler_params=_params(est, 3),
        name="fox_prompt",
    )(q_p, k_bf, v_bf, c_p, c_p)


def _ck_past_body(pt_ref, *refs, n_group, n_heads, dec_seq):
    lf_refs = refs[:n_group]
    out_ref, carry_ref = refs[n_group:]
    p = pl.program_id(1)

    @pl.when(p == 0)
    def _():
        carry_ref[...] = jnp.zeros_like(carry_ref)

    expand = (lax.broadcasted_iota(jnp.int32, (n_heads, LANES), 1) // dec_seq
              == lax.broadcasted_iota(jnp.int32, (n_heads, LANES), 0)).astype(BF16)
    later = (lax.broadcasted_iota(jnp.int32, (PAGE, PAGE), 1)
             > lax.broadcasted_iota(jnp.int32, (PAGE, PAGE), 0)).astype(BF16)
    lf = jnp.concatenate([lf_refs[g][0] for g in range(n_group)], axis=0)
    lf_e, within = None, None
    for term in _split3(lf):
        e = _dot(term.astype(BF16), expand)
        e_by_lanes = jnp.concatenate([e[g * PAGE:(g + 1) * PAGE] for g in range(n_group)], axis=1)
        w = _dot(later, e_by_lanes.astype(BF16))
        lf_e = e if lf_e is None else lf_e + e
        within = w if within is None else within + w
    carry = carry_ref[...]
    for g in range(n_group):
        within_g = within[:, g * LANES:(g + 1) * LANES]
        out_ref[0, n_group - 1 - g] = -(within_g + carry)
        carry = carry + within_g[0:1, :] + lf_e[g * PAGE:g * PAGE + 1, :]
    carry_ref[...] = carry


def _ck_past(cache_logf, page_table, dec_seq, n_group=8):
    n_seq, n_pages = page_table.shape
    n_heads = cache_logf.shape[2]
    assert n_heads * dec_seq == LANES and n_pages % n_group == 0
    lf_map = lambda b, p, pt, g: (pt[b, n_pages - 1 - (p * n_group + g)], 0, 0)
    n_steps = n_pages // n_group
    grid_spec = pltpu.PrefetchScalarGridSpec(
        num_scalar_prefetch=1,
        grid=(n_seq, n_steps),
        in_specs=[pl.BlockSpec((1, PAGE, n_heads), functools.partial(lf_map, g=g))
                  for g in range(n_group)],
        out_specs=pl.BlockSpec((1, n_group, PAGE, LANES), lambda b, p, pt: (b, n_steps - 1 - p, 0, 0)),
        scratch_shapes=[pltpu.VMEM((1, LANES), F32)])
    return pl.pallas_call(
        functools.partial(_ck_past_body, n_group=n_group, n_heads=n_heads, dec_seq=dec_seq),
        grid_spec=grid_spec,
        out_shape=jax.ShapeDtypeStruct((n_seq, n_pages, PAGE, LANES), F32),
        compiler_params=_params(64 * PAGE * LANES * 4, 2),
        name="ck_past",
    )(page_table, *([cache_logf] * n_group))


def _fox_sample_body(pt_ref, *refs, n_group, n_heads, dec_seq):
    k_refs = refs[:n_group]
    v_refs = refs[n_group:2 * n_group]
    ck_ref, qt_ref, kn_ref, vn_ref, lfn_ref, o_ref, m_ref, l_ref, acc_ref, cq_ref = refs[2 * n_group:]
    p = pl.program_id(1)
    n_steps = pl.num_programs(1)
    qt = qt_ref[0]
    eye = (lax.broadcasted_iota(jnp.int32, (LANES, LANES), 0)
           == lax.broadcasted_iota(jnp.int32, (LANES, LANES), 1))
    head_match = (lax.broadcasted_iota(jnp.int32, (n_heads, LANES), 1) // dec_seq
                  == lax.broadcasted_iota(jnp.int32, (n_heads, LANES), 0))

    def to_column(row):
        return jnp.sum(jnp.where(eye, row, 0.0), axis=-1, keepdims=True)

    def scores(rows_ref, bias, keep):
        s = _dot(rows_ref[0].astype(BF16), qt)
        s = s.reshape(bias.shape[0], n_heads, LANES) + bias[:, None, :]
        return jnp.where(keep, s, NEG_INF)

    def col_max(s3):
        return jnp.max(jnp.max(s3, axis=0), axis=0, keepdims=True)

    def col_sum(p3):
        return jnp.sum(jnp.sum(p3, axis=0), axis=0, keepdims=True)

    def weighted_values(p3, rows_ref):
        pr = p3.reshape(p3.shape[0] * n_heads, LANES).astype(BF16)
        return lax.dot_general(pr, rows_ref[0].astype(BF16), (((0,), (0,)), ((), ())),
                               preferred_element_type=F32)

    @pl.when(p == 0)
    def _():
        expand = (lax.broadcasted_iota(jnp.int32, (LANES, LANES), 1) // dec_seq
                  == lax.broadcasted_iota(jnp.int32, (LANES, LANES), 0)).astype(F32)
        lf_e = _dot_exact(lfn_ref[...], expand)
        rows = [lf_e[0:1, :]]
        for j in range(1, dec_seq):
            rows.append(rows[-1] + lf_e[j:j + 1, :])
        c_new = jnp.concatenate(rows, axis=0)
        key_j = lax.broadcasted_iota(jnp.int32, (dec_seq, LANES), 0)
        col_t = lax.broadcasted_iota(jnp.int32, (dec_seq, LANES), 1) % dec_seq
        cq = jnp.sum(jnp.where(key_j == col_t, c_new, 0.0), axis=0, keepdims=True)
        cq_ref[...] = cq
        causal = (key_j <= col_t)[:, None, :]
        s3 = scores(kn_ref, (cq - c_new) * LOG2E, jnp.logical_and(causal, head_match[None]))
        m = col_max(s3)
        p3 = jnp.exp2(s3 - m[None])
        m_ref[...] = m
        l_ref[...] = col_sum(p3)
        acc_ref[...] = weighted_values(p3, vn_ref)

    cq = cq_ref[...]
    s_pages = [scores(k_refs[g], (cq - ck_ref[0, n_group - 1 - g]) * LOG2E, head_match[None])
               for g in range(n_group)]
    m_old = m_ref[...]
    m_new = m_old
    for s3 in s_pages:
        m_new = jnp.maximum(m_new, col_max(s3))
    alpha = jnp.exp2(m_old - m_new)
    l_new = alpha * l_ref[...]
    pv = None
    for g in range(n_group):
        p3 = jnp.exp2(s_pages[g] - m_new[None])
        l_new = l_new + col_sum(p3)
        t = weighted_values(p3, v_refs[g])
        pv = t if pv is None else pv + t
    m_ref[...] = m_new
    l_ref[...] = l_new
    acc_ref[...] = to_column(alpha) * acc_ref[...] + pv

    @pl.when(p == n_steps - 1)
    def _():
        out = acc_ref[...] / to_column(l_ref[...])
        for h in range(n_heads):
            o_ref[:, h * HEAD_DIM:(h + 1) * HEAD_DIM] = out[h * dec_seq:(h + 1) * dec_seq, :]


def _fox_sample(q_s, k_s, v_s, lf_s, cache_k, cache_v, ck_pages, page_table, dec_seq, n_group=8):
    n_seq, n_pages = page_table.shape
    n_heads = cache_k.shape[2]
    width = n_heads * HEAD_DIM
    page_rows = PAGE * n_heads
    new_rows = dec_seq * n_heads
    ck_rows = cache_k.reshape(cache_k.shape[0], page_rows, HEAD_DIM)
    cv_rows = cache_v.reshape(cache_v.shape[0], page_rows, HEAD_DIM)
    kn_rows = k_s.reshape(n_seq, new_rows, HEAD_DIM)
    vn_rows = v_s.reshape(n_seq, new_rows, HEAD_DIM)
    qt = jnp.transpose(q_s.reshape(n_seq, dec_seq, n_heads, HEAD_DIM), (0, 3, 2, 1))
    qt = qt.reshape(n_seq, HEAD_DIM, n_heads * dec_seq).astype(BF16)
    page_map = lambda b, p, pt, g: (pt[b, n_pages - 1 - (p * n_group + g)], 0, 0)
    n_steps = n_pages // n_group
    seq3 = lambda b, p, pt: (b, 0, 0)
    page_block = (1, page_rows, HEAD_DIM)
    in_specs = ([pl.BlockSpec(page_block, functools.partial(page_map, g=g)) for g in range(n_group)]
                + [pl.BlockSpec(page_block, functools.partial(page_map, g=g)) for g in range(n_group)]
                + [pl.BlockSpec((1, n_group, PAGE, LANES), lambda b, p, pt: (b, n_steps - 1 - p, 0, 0)),
                   pl.BlockSpec((1, HEAD_DIM, LANES), seq3),
                   pl.BlockSpec((1, new_rows, HEAD_DIM), seq3), pl.BlockSpec((1, new_rows, HEAD_DIM), seq3),
                   pl.BlockSpec((dec_seq, LANES), lambda b, p, pt: (b, 0))])
    grid_spec = pltpu.PrefetchScalarGridSpec(
        num_scalar_prefetch=1,
        grid=(n_seq, n_steps),
        in_specs=in_specs,
        out_specs=pl.BlockSpec((dec_seq, width), lambda b, p, pt: (b, 0)),
        scratch_shapes=[pltpu.VMEM((1, LANES), F32), pltpu.VMEM((1, LANES), F32),
                        pltpu.VMEM((LANES, HEAD_DIM), F32), pltpu.VMEM((1, LANES), F32)])
    est = 2 * 2 * n_group * page_rows * HEAD_DIM * 4 + n_group * page_rows * LANES * 12
    return pl.pallas_call(
        functools.partial(_fox_sample_body, n_group=n_group, n_heads=n_heads, dec_seq=dec_seq),
        grid_spec=grid_spec,
        out_shape=jax.ShapeDtypeStruct((n_seq * dec_seq, width), F32),
        compiler_params=_params(est, 2),
        name="fox_sample",
    )(page_table, *([ck_rows] * n_group), *([cv_rows] * n_group), ck_pages, qt, kn_rows, vn_rows, lf_s)


def kernel(x_prompt, x_sample, cache_k, cache_v, cache_logf, page_table, g_pre, g_post, w_ffn_gate, w_ffn_up, w_ffn_down, w_a_in, a_ln_g, a_ln_b, w_a_spatial, b_a_spatial, w_a_out, g_kv, w_kvf, b_f, w_q, w_o):
    batch, seq, d_model = x_prompt.shape
    dec_batch, dec_seq, _ = x_sample.shape
    depth = g_pre.shape[0]
    n_a = w_a_in.shape[0]
    d_ff = w_ffn_gate.shape[-1]
    d_u = a_ln_g.shape[-1]
    n_heads = b_f.shape[0]
    width = n_heads * HEAD_DIM
    q_factor = LOG2E * HEAD_DIM ** -0.5

    x_p = x_prompt.reshape(batch * seq, d_model)
    x_s = x_sample.reshape(dec_batch * dec_seq, d_model)
    h_p, h_s = _prenorm(x_p, x_s, g_pre[0, 0])

    def ffn(x_p, x_s, h_p, h_s, l, j, g_next):
        a_p, a_s = _mm_ws(h_p, h_s, [(w_ffn_gate, (l, j), 0), (w_ffn_up, (l, j), 0)], d_ff,
                          _swiglu_epilogue, [(BF16, BF16)], name="ffn_gate_up")
        return _mm_res(a_p, a_s, w_ffn_down, (l, j), x_p, x_s, g_post[l, 2 * j], g_next,
                       0.5, bm=256, bk=512, name="ffn_down")

    sgu_p, sgu_s = [], []
    kv_out = None
    for l in range(depth):
        if l == n_a:
            k_p, k_s, kb_p, _, v_p, v_s, vb_p, _ = _mm_ws(
                hk_p, hk_s, [(w_kvf, (), 0), (w_kvf, (), width)], width, _kv_epilogue,
                [(F32, F32), (BF16, BF16), (F32, F32), (BF16, BF16)], name="proj_kv")
            lf_p, lf_s = _logf(x_p, x_s, g_kv, w_kvf[:, 2 * width:], b_f)
            c_p = _cumsum_rows(lf_p, batch)
            ck_pages = _ck_past(cache_logf, page_table, dec_seq)
            kv_out = (k_p, k_s, v_p, v_s, lf_p, lf_s)

        x_p, x_s, h_p, h_s = ffn(x_p, x_s, h_p, h_s, l, 0, g_pre[l, 1:2])

        if l < n_a:
            u_p, u_s, v_p_raw, v_s_raw = _mm_ws(
                h_p, h_s, [(w_a_in, (l,), 0), (w_a_in, (l,), d_u)], d_u, _gelu2_epilogue,
                [(BF16, BF16), (F32, F32)], name="sgu_in")
            m_p, m_s, vt_p, vt_s = _sgu_mix(u_p, v_p_raw, u_s, v_s_raw, a_ln_g[l], a_ln_b[l],
                                            w_a_spatial[l], b_a_spatial[l], batch, dec_batch)
            sgu_p.append(vt_p.reshape(batch, CHUNK, d_u))
            sgu_s.append(vt_s.reshape(dec_batch, dec_seq, d_u))
            w_mix, lead, bk = w_a_out, (l,), 512
        else:
            b = l - n_a
            q_p, q_s = _mm_ws(h_p, h_s, [(w_q, (b,), 0)], width,
                              functools.partial(_scaled_epilogue, factor=q_factor), [(BF16, F32)],
                              name="proj_q")
            m_p = _fox_prompt(q_p, kb_p, vb_p, c_p, batch)
            m_s = _fox_sample(q_s, k_s, v_s, lf_s, cache_k, cache_v, ck_pages, page_table, dec_seq)
            w_mix, lead, bk = w_o, (b,), 512
        x_p, x_s, h_p, h_s = _mm_res(m_p, m_s, w_mix, lead, x_p, x_s, g_post[l, 1], g_pre[l, 2:3], 1.0,
                                     bm=256 if l < n_a else 512, bk=bk, name="mixer_out")

        if l + 1 < depth:
            g_next = g_pre[l + 1, 0:1]
            if l + 1 == n_a:
                g_next = jnp.concatenate([g_next, g_kv.reshape(1, d_model)], axis=0)
            outs = ffn(x_p, x_s, h_p, h_s, l, 1, g_next)
            x_p, x_s, h_p, h_s = outs[:4]
            if l + 1 == n_a:
                hk_p, hk_s = outs[4:6]
        else:
            x_p, x_s = ffn(x_p, x_s, h_p, h_s, l, 1, None)

    k_p, k_s, v_p, v_s, lf_p, lf_s = kv_out
    return (x_p.reshape(batch, seq, d_model),
            x_s.reshape(dec_batch, dec_seq, d_model),
            k_p.reshape(batch, seq, n_heads, HEAD_DIM),
            v_p.reshape(batch, seq, n_heads, HEAD_DIM),
            lf_p[:, :n_heads].reshape(batch, seq, n_heads),
            k_s.reshape(dec_batch, dec_seq, n_heads, HEAD_DIM),
            v_s.reshape(dec_batch, dec_seq, n_heads, HEAD_DIM),
            lf_s[:, :n_heads].reshape(dec_batch, dec_seq, n_heads),
            jnp.stack(sgu_p),
            jnp.stack(sgu_s))
```

```python
import functools

import jax
import jax.numpy as jnp
from jax import lax
from jax.experimental import pallas as pl
from jax.experimental.pallas import tpu as pltpu

F32 = jnp.float32
BF16 = jnp.bfloat16

RMS_EPS = 1e-6
LN_EPS = 1e-5
NEG_INF = -1e30
LOG2E = 1.4426950408889634
CHUNK = 128
N_GROUPS = 16
HEAD_DIM = 128
PAGE = 128

V7X_VMEM_LIMIT_CAP = 56 * 1024 * 1024
LANES = 128


def _vmem_limit(estimate_bytes):
    return int(min(max(estimate_bytes * 5 // 4, 16 * 1024 * 1024), V7X_VMEM_LIMIT_CAP))


def _params(estimate_bytes, n_grid_dims):
    return pltpu.CompilerParams(
        dimension_semantics=("arbitrary",) * n_grid_dims,
        vmem_limit_bytes=_vmem_limit(estimate_bytes))


def _rms(x, g):
    return x * lax.rsqrt(jnp.mean(x * x, axis=-1, keepdims=True) + RMS_EPS) * g


def _dot(a, b):
    return jnp.dot(a, b, preferred_element_type=F32)


def _dot_exact(a, b):
    return jnp.dot(a, b, preferred_element_type=F32, precision=lax.Precision.HIGHEST)


def _weight_spec(w, lead, block, index):
    n_lead = len(lead)
    assert w.ndim == n_lead + 2
    return pl.BlockSpec((None,) * n_lead + block, lambda *g: tuple(lead) + index(*g))


def _prenorm_body(xp_ref, xs_ref, g_ref, hp_ref, hs_ref, *, n_tiles):
    i = pl.program_id(0)

    @pl.when(i < n_tiles)
    def _():
        hp_ref[...] = _rms(xp_ref[...], g_ref[...]).astype(BF16)

    @pl.when(i == n_tiles)
    def _():
        hs_ref[...] = _rms(xs_ref[...], g_ref[...]).astype(BF16)


def _prenorm(x_p, x_s, g, bm=512):
    mp, d = x_p.shape
    ms = x_s.shape[0]
    n_tiles = mp // bm
    last = n_tiles - 1
    row = lambda i: (jnp.minimum(i, last), 0)
    fixed = lambda i: (0, 0)
    return pl.pallas_call(
        functools.partial(_prenorm_body, n_tiles=n_tiles),
        grid=(n_tiles + 1,),
        in_specs=[pl.BlockSpec((bm, d), row), pl.BlockSpec((ms, d), fixed),
                  pl.BlockSpec((1, d), fixed)],
        out_specs=[pl.BlockSpec((bm, d), row), pl.BlockSpec((ms, d), fixed)],
        out_shape=[jax.ShapeDtypeStruct((mp, d), BF16), jax.ShapeDtypeStruct((ms, d), BF16)],
        compiler_params=_params(6 * bm * d * 4, 1),
        name="prenorm",
    )(x_p, x_s, g.reshape(1, d))


def _mm_ws_body(*refs, n_w, n_out, epilogue):
    hp_ref, hs_ref = refs[0], refs[1]
    w_refs = refs[2:2 + n_w]
    out_refs = refs[2 + n_w:2 + n_w + 2 * n_out]
    wbf_refs = refs[2 + n_w + 2 * n_out:]
    m = pl.program_id(1)

    def run(h_ref, o_refs):
        h = h_ref[...]
        outs = epilogue(*[_dot(h, wbf_ref[...]) for wbf_ref in wbf_refs])
        for o_ref, o in zip(o_refs, outs):
            o_ref[...] = o.astype(o_ref.dtype)

    @pl.when(m == 0)
    def _():
        for w_ref, wbf_ref in zip(w_refs, wbf_refs):
            wbf_ref[...] = w_ref[...].astype(BF16)
        run(hs_ref, out_refs[1::2])

    @pl.when(m > 0)
    def _():
        run(hp_ref, out_refs[0::2])


def _mm_ws(h_p, h_s, weights, n_cols, epilogue, out_dtypes, *, bm=1024, bn=512, name):
    mp, k = h_p.shape
    ms = h_s.shape[0]
    n_tiles = mp // bm
    n_w, n_out = len(weights), len(out_dtypes)
    prow = lambda m: jnp.maximum(m - 1, 0)
    in_specs = [pl.BlockSpec((bm, k), lambda n, m: (prow(m), 0)),
                pl.BlockSpec((ms, k), lambda n, m: (0, 0))]
    for w, lead, off in weights:
        assert off % bn == 0
        in_specs.append(_weight_spec(w, lead, (k, bn), lambda n, m, o=off // bn: (0, n + o)))
    out_specs, out_shape = [], []
    for dt_p, dt_s in out_dtypes:
        out_specs += [pl.BlockSpec((bm, bn), lambda n, m: (prow(m), n)),
                      pl.BlockSpec((ms, bn), lambda n, m: (0, n))]
        out_shape += [jax.ShapeDtypeStruct((mp, n_cols), dt_p),
                      jax.ShapeDtypeStruct((ms, n_cols), dt_s)]
    est = (2 * bm * k * 2 + n_w * k * bn * (2 * 4 + 2)
           + n_out * 2 * bm * bn * 4 + (n_w + n_out) * bm * bn * 4)
    return pl.pallas_call(
        functools.partial(_mm_ws_body, n_w=n_w, n_out=n_out, epilogue=epilogue),
        grid=(n_cols // bn, n_tiles + 1),
        in_specs=in_specs, out_specs=out_specs, out_shape=out_shape,
        scratch_shapes=[pltpu.VMEM((k, bn), BF16) for _ in weights],
        compiler_params=_params(est, 2),
        name=name,
    )(h_p, h_s, *[w for w, _, _ in weights])


def _swiglu_epilogue(g, u):
    return (g / (1.0 + jnp.exp(-g)) * u,)


def _gelu(z):
    return 0.5 * z * (1.0 + lax.erf(z * (2.0 ** -0.5)))


def _gelu2_epilogue(zu, zv):
    return (_gelu(zu), _gelu(zv))


def _kv_epilogue(zk, zv):
    return (zk, zk, zv, zv)


def _scaled_epilogue(z, *, factor):
    return (z * factor,)


def _mm_res_body(*refs, n_h, n_k, bk, n_tiles, coef):
    ap_ref, as_ref, w_ref, xp_ref, xs_ref, gpost_ref, gnext_ref = refs[:7]
    out_refs = refs[7:7 + 2 * (1 + n_h)]
    wbf_ref, y0_ref = refs[7 + 2 * (1 + n_h):]
    i = pl.program_id(0)

    @pl.when(i < n_k)
    def _():
        r0 = pl.multiple_of(i * bk, bk)
        w_chunk = w_ref[...].astype(BF16)
        wbf_ref[pl.ds(r0, bk), :] = w_chunk
        part = _dot(ap_ref[:, pl.ds(r0, bk)].astype(BF16), w_chunk)

        @pl.when(i == 0)
        def _():
            y0_ref[...] = part

        @pl.when(i > 0)
        def _():
            y0_ref[...] += part

    def run(a_ref, x_ref, o_refs, y=None):
        if y is None:
            y = _dot(a_ref[...].astype(BF16), wbf_ref[...])
        x_new = x_ref[...] + coef * _rms(y, gpost_ref[...])
        o_refs[0][...] = x_new
        if n_h:
            xn = x_new * lax.rsqrt(jnp.mean(x_new * x_new, axis=-1, keepdims=True) + RMS_EPS)
            for j in range(n_h):
                o_refs[1 + j][...] = (xn * gnext_ref[j:j + 1, :]).astype(BF16)

    @pl.when(i == n_k)
    def _():
        run(ap_ref, xp_ref, out_refs[0::2], y0_ref[...])

    @pl.when(jnp.logical_and(i > n_k, i < n_k + n_tiles))
    def _():
        run(ap_ref, xp_ref, out_refs[0::2])

    @pl.when(i == n_k + n_tiles)
    def _():
        run(as_ref, xs_ref, out_refs[1::2])


def _mm_res(a_p, a_s, w, lead, x_p, x_s, g_post, g_next, coef, *, bm, bk, name):
    mp, k = a_p.shape
    ms = a_s.shape[0]
    d = w.shape[-1]
    n_h = 0 if g_next is None else g_next.shape[0]
    g_next_arr = jnp.zeros((1, d), F32) if g_next is None else g_next
    n_tiles = mp // bm
    n_k = k // bk
    assert n_k * bk == k
    row = lambda i: (jnp.clip(i - n_k, 0, n_tiles - 1), 0)
    fixed = lambda i: (0, 0)
    in_specs = [pl.BlockSpec((bm, k), row), pl.BlockSpec((ms, k), fixed),
                _weight_spec(w, lead, (bk, d), lambda i: (jnp.minimum(i, n_k - 1), 0)),
                pl.BlockSpec((bm, d), row), pl.BlockSpec((ms, d), fixed),
                pl.BlockSpec((1, d), fixed), pl.BlockSpec(g_next_arr.shape, fixed)]
    out_specs = [pl.BlockSpec((bm, d), row), pl.BlockSpec((ms, d), fixed)]
    out_shape = [jax.ShapeDtypeStruct((mp, d), F32), jax.ShapeDtypeStruct((ms, d), F32)]
    for _ in range(n_h):
        out_specs += [pl.BlockSpec((bm, d), row), pl.BlockSpec((ms, d), fixed)]
        out_shape += [jax.ShapeDtypeStruct((mp, d), BF16), jax.ShapeDtypeStruct((ms, d), BF16)]
    est = (k * d * 2 + 2 * bk * d * 4 + 2 * bm * k * a_p.dtype.itemsize + 4 * bm * d * 4
           + n_h * 2 * bm * d * 2 + 3 * bm * d * 4)
    return pl.pallas_call(
        functools.partial(_mm_res_body, n_h=n_h, n_k=n_k, bk=bk, n_tiles=n_tiles, coef=coef),
        grid=(n_k + n_tiles + 1,),
        in_specs=in_specs, out_specs=out_specs, out_shape=out_shape,
        scratch_shapes=[pltpu.VMEM((k, d), BF16), pltpu.VMEM((bm, d), F32)],
        compiler_params=_params(est, 1),
        name=name,
    )(a_p, a_s, w, x_p, x_s, g_post.reshape(1, d), g_next_arr)


def _sgu_body(up_ref, vp_ref, us_ref, vs_ref, lng_ref, lnb_ref, wp_ref, bp_ref, ws_ref, bs_ref,
              pp_ref, ps_ref, vtp_ref, vts_ref, *, n_chunks):
    c = pl.program_id(0)

    def run(u_ref, v_ref, w_ref, b_ref, p_ref, vt_ref):
        rows = v_ref.shape[0]
        gw = v_ref.shape[1] // N_GROUPS
        v = v_ref[...]
        xc = v - jnp.mean(v, axis=-1, keepdims=True)
        var = jnp.mean(xc * xc, axis=-1, keepdims=True)
        vn = xc * lax.rsqrt(var + LN_EPS) * lng_ref[...] + lnb_ref[...]
        vt_ref[...] = vn
        vb = vn.astype(BF16)
        causal = (lax.broadcasted_iota(jnp.int32, (rows, rows), 0)
                  >= lax.broadcasted_iota(jnp.int32, (rows, rows), 1))
        for g in range(N_GROUPS):
            wm = jnp.where(causal, w_ref[g], 0.0).astype(BF16)
            mixed = _dot(wm, vb[:, g * gw:(g + 1) * gw]) + b_ref[:, g:g + 1]
            p_ref[:, g * gw:(g + 1) * gw] = (
                u_ref[:, g * gw:(g + 1) * gw].astype(F32) * mixed).astype(BF16)

    @pl.when(c < n_chunks)
    def _():
        run(up_ref, vp_ref, wp_ref, bp_ref, pp_ref, vtp_ref)

    @pl.when(c == n_chunks)
    def _():
        run(us_ref, vs_ref, ws_ref, bs_ref, ps_ref, vts_ref)


def _sgu_mix(u_p, v_p, u_s, v_s, ln_g, ln_b, w_sp, b_sp, batch, dec_batch):
    mp, du = u_p.shape
    ms = u_s.shape[0]
    n_chunks = mp // CHUNK
    chunks_per_seq = n_chunks // batch
    dec_seq = ms // dec_batch
    last = n_chunks - 1
    eye = jnp.eye(dec_batch, dtype=F32)
    w_s = (eye[None, :, None, :, None] * w_sp[:, None, :dec_seq, None, :dec_seq]
           ).reshape(N_GROUPS, ms, ms)
    b_s = jnp.tile(jnp.transpose(b_sp[:, :dec_seq]), (dec_batch, 1))
    b_p = jnp.transpose(b_sp)
    row = lambda c: (jnp.minimum(c, last), 0)
    fixed2 = lambda c: (0, 0)
    fixed3 = lambda c: (0, 0, 0)
    est = 2 * CHUNK * du * (2 + 4 + 2 + 4) + 8 * CHUNK * du * 4
    return pl.pallas_call(
        functools.partial(_sgu_body, n_chunks=n_chunks),
        grid=(n_chunks + 1,),
        in_specs=[pl.BlockSpec((CHUNK, du), row), pl.BlockSpec((CHUNK, du), row),
                  pl.BlockSpec((ms, du), fixed2), pl.BlockSpec((ms, du), fixed2),
                  pl.BlockSpec((1, du), fixed2), pl.BlockSpec((1, du), fixed2),
                  pl.BlockSpec((N_GROUPS, CHUNK, CHUNK), fixed3), pl.BlockSpec((CHUNK, N_GROUPS), fixed2),
                  pl.BlockSpec((N_GROUPS, ms, ms), fixed3), pl.BlockSpec((ms, N_GROUPS), fixed2)],
        out_specs=[pl.BlockSpec((CHUNK, du), row), pl.BlockSpec((ms, du), fixed2),
                   pl.BlockSpec((CHUNK, du), lambda c: (jnp.minimum(c, last) // chunks_per_seq, 0)),
                   pl.BlockSpec((ms, du), fixed2)],
        out_shape=[jax.ShapeDtypeStruct((mp, du), BF16), jax.ShapeDtypeStruct((ms, du), BF16),
                   jax.ShapeDtypeStruct((batch * CHUNK, du), F32), jax.ShapeDtypeStruct((ms, du), F32)],
        compiler_params=_params(est, 1),
        name="sgu_mix",
    )(u_p, v_p, u_s, v_s, ln_g.reshape(1, du), ln_b.reshape(1, du), w_sp, b_p, w_s, b_s)


def _logf_body(xp_ref, xs_ref, g_ref, w_ref, b_ref, lp_ref, ls_ref, *, n_tiles):
    i = pl.program_id(0)

    def run(x_ref, o_ref):
        z = _dot_exact(_rms(x_ref[...], g_ref[...]), w_ref[...]) + b_ref[...]
        o_ref[...] = jnp.minimum(z, 0.0) - jnp.log1p(jnp.exp(-jnp.abs(z)))

    @pl.when(i < n_tiles)
    def _():
        run(xp_ref, lp_ref)

    @pl.when(i == n_tiles)
    def _():
        run(xs_ref, ls_ref)


def _logf(x_p, x_s, g_kv, w_f, b_f, bm=512):
    mp, d = x_p.shape
    ms = x_s.shape[0]
    nh = w_f.shape[1]
    w_pad = jnp.pad(w_f, ((0, 0), (0, LANES - nh)))
    b_pad = jnp.pad(b_f, (0, LANES - nh)).reshape(1, LANES)
    n_tiles = mp // bm
    last = n_tiles - 1
    row = lambda i: (jnp.minimum(i, last), 0)
    fixed = lambda i: (0, 0)
    return pl.pallas_call(
        functools.partial(_logf_body, n_tiles=n_tiles),
        grid=(n_tiles + 1,),
        in_specs=[pl.BlockSpec((bm, d), row), pl.BlockSpec((ms, d), fixed),
                  pl.BlockSpec((1, d), fixed), pl.BlockSpec((d, LANES), fixed),
                  pl.BlockSpec((1, LANES), fixed)],
        out_specs=[pl.BlockSpec((bm, LANES), row), pl.BlockSpec((ms, LANES), fixed)],
        out_shape=[jax.ShapeDtypeStruct((mp, LANES), F32), jax.ShapeDtypeStruct((ms, LANES), F32)],
        compiler_params=_params(8 * bm * d * 4, 1),
        name="logf",
    )(x_p, x_s, g_kv.reshape(1, d), w_pad, b_pad)


def _cumsum_body(l_ref, c_ref, *, n_chunks):
    tril = (lax.broadcasted_iota(jnp.int32, (CHUNK, CHUNK), 0)
            >= lax.broadcasted_iota(jnp.int32, (CHUNK, CHUNK), 1)).astype(F32)

    def step(j, carry):
        r0 = pl.multiple_of(j * CHUNK, CHUNK)
        local = _dot_exact(tril, l_ref[pl.ds(r0, CHUNK), :]) + carry
        c_ref[pl.ds(r0, CHUNK), :] = local
        return local[CHUNK - 1:CHUNK, :]

    lax.fori_loop(0, n_chunks, step, jnp.zeros((1, LANES), F32))


def _cumsum_rows(lf_p, batch):
    mp = lf_p.shape[0]
    seq = mp // batch
    return pl.pallas_call(
        functools.partial(_cumsum_body, n_chunks=seq // CHUNK),
        grid=(batch,),
        in_specs=[pl.BlockSpec((seq, LANES), lambda b: (b, 0))],
        out_specs=pl.BlockSpec((seq, LANES), lambda b: (b, 0)),
        out_shape=jax.ShapeDtypeStruct((mp, LANES), F32),
        compiler_params=_params(4 * seq * LANES * 4, 1),
        name="cumsum_logf",
    )(lf_p)


def _head_column(c, h):
    lane = lax.broadcasted_iota(jnp.int32, (1, LANES), 1)
    return jnp.sum(jnp.where(lane == h, c, 0.0), axis=-1, keepdims=True) * LOG2E


def _split3(c):
    c1 = c.astype(BF16).astype(F32)
    r1 = c - c1
    c2 = r1.astype(BF16).astype(F32)
    c3 = (r1 - c2).astype(BF16).astype(F32)
    return c1, c2, c3


def _bias_columns(c_col, query_side):
    rows = c_col.shape[0]
    lane = lax.broadcasted_iota(jnp.int32, (rows, LANES), 1)
    c1, c2, c3 = _split3(c_col)
    if query_side:
        aug = jnp.where(lane == 0, c1, jnp.where(lane == 1, c2, jnp.where(lane == 2, c3, jnp.where(lane < 6, 1.0, 0.0))))
    else:
        aug = jnp.where(lane < 3, 1.0, jnp.where(lane == 3, -c1, jnp.where(lane == 4, -c2, jnp.where(lane == 5, -c3, 0.0))))
    return aug.astype(BF16)


def _fox_prompt_step(hg, qi, q_ref, k_ref, v_ref, cq_ref, ck_ref, o_ref, kaug_ref, vt_ref, *, tq, tk, hs, nq):
    head_cols = [slice(a * HEAD_DIM, (a + 1) * HEAD_DIM) for a in range(hs)]
    aug_cols = [slice(a * 2 * HEAD_DIM, (a + 1) * 2 * HEAD_DIM) for a in range(hs)]

    @pl.when(qi == 0)
    def _():
        ck = ck_ref[...]
        for a in range(hs):
            kaug_ref[:, aug_cols[a]] = jnp.concatenate(
                [k_ref[:, head_cols[a]], _bias_columns(_head_column(ck, hg * hs + a), False)], axis=1)
            vt_ref[a] = v_ref[:, head_cols[a]].astype(F32).T.astype(BF16)

    cq = cq_ref[...]
    q_t = [jnp.concatenate([q_ref[:, head_cols[a]].astype(F32),
                            _bias_columns(_head_column(cq, hg * hs + a), True).astype(F32)],
                           axis=1).T.astype(BF16) for a in range(hs)]

    def tile(j, carry, masked):
        r0 = j * tk
        out = []
        for a in range(hs):
            m, l, acc = carry[a]
            s = _dot(kaug_ref[pl.ds(r0, tk), aug_cols[a]], q_t[a])
            if masked:
                s = jnp.where(lax.broadcasted_iota(jnp.int32, (tk, tq), 0)
                              <= lax.broadcasted_iota(jnp.int32, (tk, tq), 1), s, NEG_INF)
            m_new = jnp.maximum(m, jnp.max(s, axis=0, keepdims=True))
            alpha = jnp.exp2(m - m_new)
            p = jnp.exp2(s - m_new)
            l = alpha * l + jnp.sum(p, axis=0, keepdims=True)
            acc = alpha * acc + _dot(vt_ref[a, :, pl.ds(r0, tk)], p.astype(BF16))
            out.append((m_new, l, acc))
        return tuple(out)

    for n_full in range(nq):
        @pl.when(qi == n_full)
        def _(n_full=n_full):
            carry = tuple((jnp.full((1, tq), NEG_INF, F32), jnp.zeros((1, tq), F32),
                           jnp.zeros((HEAD_DIM, tq), F32)) for _ in range(hs))
            for j in range(n_full):
                carry = tile(j, carry, False)
            carry = tile(n_full, carry, True)
            for a in range(hs):
                _, l, acc = carry[a]
                o_ref[:, head_cols[a]] = (acc / l).T.astype(o_ref.dtype)


def _ck_past_body(pt_ref, *refs, n_group, n_heads, dec_seq):
    lf_refs = refs[:n_group]
    out_ref, carry_ref = refs[n_group:]
    p = pl.program_id(1)

    @pl.when(p == 0)
    def _():
        carry_ref[...] = jnp.zeros_like(carry_ref)

    expand = (lax.broadcasted_iota(jnp.int32, (n_heads, LANES), 1) // dec_seq
              == lax.broadcasted_iota(jnp.int32, (n_heads, LANES), 0)).astype(BF16)
    later = (lax.broadcasted_iota(jnp.int32, (PAGE, PAGE), 1)
             > lax.broadcasted_iota(jnp.int32, (PAGE, PAGE), 0)).astype(BF16)
    lf = jnp.concatenate([lf_refs[g][0] for g in range(n_group)], axis=0)
    lf_e, within = None, None
    for term in _split3(lf):
        e = _dot(term.astype(BF16), expand)
        e_by_lanes = jnp.concatenate([e[g * PAGE:(g + 1) * PAGE] for g in range(n_group)], axis=1)
        w = _dot(later, e_by_lanes.astype(BF16))
        lf_e = e if lf_e is None else lf_e + e
        within = w if within is None else within + w
    carry = carry_ref[...]
    for g in range(n_group):
        within_g = within[:, g * LANES:(g + 1) * LANES]
        out_ref[0, n_group - 1 - g] = -(within_g + carry)
        carry = carry + within_g[0:1, :] + lf_e[g * PAGE:g * PAGE + 1, :]
    carry_ref[...] = carry


def _ck_past(cache_logf, page_table, dec_seq, n_group=8):
    n_seq, n_pages = page_table.shape
    n_heads = cache_logf.shape[2]
    assert n_heads * dec_seq == LANES and n_pages % n_group == 0
    lf_map = lambda b, p, pt, g: (pt[b, n_pages - 1 - (p * n_group + g)], 0, 0)
    n_steps = n_pages // n_group
    grid_spec = pltpu.PrefetchScalarGridSpec(
        num_scalar_prefetch=1,
        grid=(n_seq, n_steps),
        in_specs=[pl.BlockSpec((1, PAGE, n_heads), functools.partial(lf_map, g=g))
                  for g in range(n_group)],
        out_specs=pl.BlockSpec((1, n_group, PAGE, LANES), lambda b, p, pt: (b, n_steps - 1 - p, 0, 0)),
        scratch_shapes=[pltpu.VMEM((1, LANES), F32)])
    return pl.pallas_call(
        functools.partial(_ck_past_body, n_group=n_group, n_heads=n_heads, dec_seq=dec_seq),
        grid_spec=grid_spec,
        out_shape=jax.ShapeDtypeStruct((n_seq, n_pages, PAGE, LANES), F32),
        compiler_params=_params(64 * PAGE * LANES * 4, 2),
        name="ck_past",
    )(page_table, *([cache_logf] * n_group))


def _fox_sample_body(pt_ref, *refs, n_group, n_heads, dec_seq):
    k_refs = refs[:n_group]
    v_refs = refs[n_group:2 * n_group]
    ck_ref, qt_ref, kn_ref, vn_ref, lfn_ref, o_ref, m_ref, l_ref, acc_ref, cq_ref = refs[2 * n_group:]
    p = pl.program_id(1)
    n_steps = pl.num_programs(1)
    qt = qt_ref[0]
    eye = (lax.broadcasted_iota(jnp.int32, (LANES, LANES), 0)
           == lax.broadcasted_iota(jnp.int32, (LANES, LANES), 1))
    head_match = (lax.broadcasted_iota(jnp.int32, (n_heads, LANES), 1) // dec_seq
                  == lax.broadcasted_iota(jnp.int32, (n_heads, LANES), 0))

    def to_column(row):
        return jnp.sum(jnp.where(eye, row, 0.0), axis=-1, keepdims=True)

    def scores(rows_ref, bias, keep):
        s = _dot(rows_ref[0].astype(BF16), qt)
        s = s.reshape(bias.shape[0], n_heads, LANES) + bias[:, None, :]
        return jnp.where(keep, s, NEG_INF)

    def col_max(s3):
        return jnp.max(jnp.max(s3, axis=0), axis=0, keepdims=True)

    def col_sum(p3):
        return jnp.sum(jnp.sum(p3, axis=0), axis=0, keepdims=True)

    def weighted_values(p3, rows_ref):
        pr = p3.reshape(p3.shape[0] * n_heads, LANES).astype(BF16)
        return lax.dot_general(pr, rows_ref[0].astype(BF16), (((0,), (0,)), ((), ())),
                               preferred_element_type=F32)

    @pl.when(p == 0)
    def _():
        expand = (lax.broadcasted_iota(jnp.int32, (LANES, LANES), 1) // dec_seq
                  == lax.broadcasted_iota(jnp.int32, (LANES, LANES), 0)).astype(F32)
        lf_e = _dot_exact(lfn_ref[...], expand)
        rows = [lf_e[0:1, :]]
        for j in range(1, dec_seq):
            rows.append(rows[-1] + lf_e[j:j + 1, :])
        c_new = jnp.concatenate(rows, axis=0)
        key_j = lax.broadcasted_iota(jnp.int32, (dec_seq, LANES), 0)
        col_t = lax.broadcasted_iota(jnp.int32, (dec_seq, LANES), 1) % dec_seq
        cq = jnp.sum(jnp.where(key_j == col_t, c_new, 0.0), axis=0, keepdims=True)
        cq_ref[...] = cq
        causal = (key_j <= col_t)[:, None, :]
        s3 = scores(kn_ref, (cq - c_new) * LOG2E, jnp.logical_and(causal, head_match[None]))
        m = col_max(s3)
        p3 = jnp.exp2(s3 - m[None])
        m_ref[...] = m
        l_ref[...] = col_sum(p3)
        acc_ref[...] = weighted_values(p3, vn_ref)

    cq = cq_ref[...]
    s_pages = [scores(k_refs[g], (cq - ck_ref[0, n_group - 1 - g]) * LOG2E, head_match[None])
               for g in range(n_group)]
    m_old = m_ref[...]
    m_new = m_old
    for s3 in s_pages:
        m_new = jnp.maximum(m_new, col_max(s3))
    alpha = jnp.exp2(m_old - m_new)
    l_new = alpha * l_ref[...]
    pv = None
    for g in range(n_group):
        p3 = jnp.exp2(s_pages[g] - m_new[None])
        l_new = l_new + col_sum(p3)
        t = weighted_values(p3, v_refs[g])
        pv = t if pv is None else pv + t
    m_ref[...] = m_new
    l_ref[...] = l_new
    acc_ref[...] = to_column(alpha) * acc_ref[...] + pv

    @pl.when(p == n_steps - 1)
    def _():
        out = acc_ref[...] / to_column(l_ref[...])
        for h in range(n_heads):
            o_ref[:, h * HEAD_DIM:(h + 1) * HEAD_DIM] = out[h * dec_seq:(h + 1) * dec_seq, :]


def _fox_body(pt_ref, *refs, n_sample_in, spp, sample_kw, prompt_kw):
    sample_in = refs[:n_sample_in]
    prompt_in = refs[n_sample_in:n_sample_in + 5]
    o_s, o_p = refs[n_sample_in + 5:n_sample_in + 7]
    sample_scratch = refs[n_sample_in + 7:n_sample_in + 11]
    prompt_scratch = refs[n_sample_in + 11:]
    step = pl.program_id(0) * pl.num_programs(1) + pl.program_id(1)
    nq, n_hg = prompt_kw["nq"], prompt_kw["n_hg"]
    step_kw = {k: v for k, v in prompt_kw.items() if k != "n_hg"}

    @pl.when(step % spp == 0)
    def _():
        j = step // spp
        _fox_prompt_step((j // nq) % n_hg, j % nq, *prompt_in, o_p, *prompt_scratch, **step_kw)

    _fox_sample_body(pt_ref, *sample_in, o_s, *sample_scratch, **sample_kw)


def _fox_attention(q_p, k_bf, v_bf, c_p, batch, q_s, k_s, v_s, lf_s, cache_k, cache_v, ck_pages, page_table,
                   dec_seq, n_group=4, tq=1024, hs=1):
    n_seq, n_pages = page_table.shape
    n_heads = cache_k.shape[2]
    width = n_heads * HEAD_DIM
    page_rows = PAGE * n_heads
    new_rows = dec_seq * n_heads
    n_steps = n_pages // n_group
    mp = q_p.shape[0]
    seq = mp // batch
    nq, n_hg, hw = seq // tq, n_heads // hs, hs * HEAD_DIM
    spp = (n_seq * n_steps) // (batch * n_hg * nq)
    assert spp * batch * n_hg * nq == n_seq * n_steps

    def prompt_idx(b, p):
        j = (b * n_steps + p) // spp
        return j // (n_hg * nq), (j // nq) % n_hg, j % nq
    ck_rows = cache_k.reshape(cache_k.shape[0], page_rows, HEAD_DIM)
    cv_rows = cache_v.reshape(cache_v.shape[0], page_rows, HEAD_DIM)
    kn_rows = k_s.reshape(n_seq, new_rows, HEAD_DIM)
    vn_rows = v_s.reshape(n_seq, new_rows, HEAD_DIM)
    qt = jnp.transpose(q_s.reshape(n_seq, dec_seq, n_heads, HEAD_DIM), (0, 3, 2, 1))
    qt = qt.reshape(n_seq, HEAD_DIM, n_heads * dec_seq).astype(BF16)
    page_map = lambda b, p, pt, g: (pt[b, n_pages - 1 - (p * n_group + g)], 0, 0)
    seq3 = lambda b, p, pt: (b, 0, 0)
    page_block = (1, page_rows, HEAD_DIM)
    sample_specs = ([pl.BlockSpec(page_block, functools.partial(page_map, g=g)) for g in range(n_group)]
                    + [pl.BlockSpec(page_block, functools.partial(page_map, g=g)) for g in range(n_group)]
                    + [pl.BlockSpec((1, n_group, PAGE, LANES), lambda b, p, pt: (b, n_steps - 1 - p, 0, 0)),
                       pl.BlockSpec((1, HEAD_DIM, LANES), seq3),
                       pl.BlockSpec((1, new_rows, HEAD_DIM), seq3), pl.BlockSpec((1, new_rows, HEAD_DIM), seq3),
                       pl.BlockSpec((dec_seq, LANES), lambda b, p, pt: (b, 0))])

    def q_tile(b, p, pt):
        pb, hg, qi = prompt_idx(b, p)
        return pb * nq + qi, hg

    def head_rows(b, p, pt):
        pb, hg, _ = prompt_idx(b, p)
        return pb, hg

    def c_tile(b, p, pt):
        pb, _, qi = prompt_idx(b, p)
        return pb * nq + qi, 0

    prompt_specs = [pl.BlockSpec((tq, hw), q_tile), pl.BlockSpec((seq, hw), head_rows),
                    pl.BlockSpec((seq, hw), head_rows), pl.BlockSpec((tq, LANES), c_tile),
                    pl.BlockSpec((seq, LANES), lambda b, p, pt: (prompt_idx(b, p)[0], 0))]
    grid_spec = pltpu.PrefetchScalarGridSpec(
        num_scalar_prefetch=1,
        grid=(n_seq, n_steps),
        in_specs=sample_specs + prompt_specs,
        out_specs=[pl.BlockSpec((dec_seq, width), lambda b, p, pt: (b, 0)), pl.BlockSpec((tq, hw), q_tile)],
        scratch_shapes=[pltpu.VMEM((1, LANES), F32), pltpu.VMEM((1, LANES), F32),
                        pltpu.VMEM((LANES, HEAD_DIM), F32), pltpu.VMEM((1, LANES), F32),
                        pltpu.VMEM((seq, 2 * hw), BF16), pltpu.VMEM((hs, HEAD_DIM, seq), BF16)])
    est = (2 * 2 * n_group * page_rows * HEAD_DIM * 4 + n_group * page_rows * LANES * 12
           + 2 * (2 * seq * hw * 2) + 3 * seq * LANES * 4 + 3 * seq * hw * 2 + hs * 6 * tq * tq * 4)
    o_s, o_p = pl.pallas_call(
        functools.partial(
            _fox_body, n_sample_in=len(sample_specs), spp=spp,
            sample_kw=dict(n_group=n_group, n_heads=n_heads, dec_seq=dec_seq),
            prompt_kw=dict(tq=tq, tk=tq, hs=hs, nq=nq, n_hg=n_hg)),
        grid_spec=grid_spec,
        out_shape=[jax.ShapeDtypeStruct((n_seq * dec_seq, width), F32),
                   jax.ShapeDtypeStruct((mp, width), BF16)],
        compiler_params=_params(est, 2),
        name="fox_attention",
    )(page_table, *([ck_rows] * n_group), *([cv_rows] * n_group), ck_pages, qt, kn_rows, vn_rows, lf_s,
      q_p, k_bf, v_bf, c_p, c_p)
    return o_p, o_s


def kernel(x_prompt, x_sample, cache_k, cache_v, cache_logf, page_table, g_pre, g_post, w_ffn_gate, w_ffn_up, w_ffn_down, w_a_in, a_ln_g, a_ln_b, w_a_spatial, b_a_spatial, w_a_out, g_kv, w_kvf, b_f, w_q, w_o):
    batch, seq, d_model = x_prompt.shape
    dec_batch, dec_seq, _ = x_sample.shape
    depth = g_pre.shape[0]
    n_a = w_a_in.shape[0]
    d_ff = w_ffn_gate.shape[-1]
    d_u = a_ln_g.shape[-1]
    n_heads = b_f.shape[0]
    width = n_heads * HEAD_DIM
    q_factor = LOG2E * HEAD_DIM ** -0.5

    x_p = x_prompt.reshape(batch * seq, d_model)
    x_s = x_sample.reshape(dec_batch * dec_seq, d_model)
    h_p, h_s = _prenorm(x_p, x_s, g_pre[0, 0])

    def ffn(x_p, x_s, h_p, h_s, l, j, g_next):
        a_p, a_s = _mm_ws(h_p, h_s, [(w_ffn_gate, (l, j), 0), (w_ffn_up, (l, j), 0)], d_ff,
                          _swiglu_epilogue, [(BF16, BF16)], name="ffn_gate_up")
        return _mm_res(a_p, a_s, w_ffn_down, (l, j), x_p, x_s, g_post[l, 2 * j], g_next,
                       0.5, bm=256, bk=512, name="ffn_down")

    sgu_p, sgu_s = [], []
    kv_out = None
    for l in range(depth):
        if l == n_a:
            k_p, k_s, kb_p, _, v_p, v_s, vb_p, _ = _mm_ws(
                hk_p, hk_s, [(w_kvf, (), 0), (w_kvf, (), width)], width, _kv_epilogue,
                [(F32, F32), (BF16, BF16), (F32, F32), (BF16, BF16)], name="proj_kv")
            lf_p, lf_s = _logf(x_p, x_s, g_kv, w_kvf[:, 2 * width:], b_f)
            c_p = _cumsum_rows(lf_p, batch)
            ck_pages = _ck_past(cache_logf, page_table, dec_seq)
            kv_out = (k_p, k_s, v_p, v_s, lf_p, lf_s)

        x_p, x_s, h_p, h_s = ffn(x_p, x_s, h_p, h_s, l, 0, g_pre[l, 1:2])

        if l < n_a:
            u_p, u_s, v_p_raw, v_s_raw = _mm_ws(
                h_p, h_s, [(w_a_in, (l,), 0), (w_a_in, (l,), d_u)], d_u, _gelu2_epilogue,
                [(BF16, BF16), (F32, F32)], name="sgu_in")
            m_p, m_s, vt_p, vt_s = _sgu_mix(u_p, v_p_raw, u_s, v_s_raw, a_ln_g[l], a_ln_b[l],
                                            w_a_spatial[l], b_a_spatial[l], batch, dec_batch)
            sgu_p.append(vt_p.reshape(batch, CHUNK, d_u))
            sgu_s.append(vt_s.reshape(dec_batch, dec_seq, d_u))
            w_mix, lead, bk = w_a_out, (l,), 512
        else:
            b = l - n_a
            q_p, q_s = _mm_ws(h_p, h_s, [(w_q, (b,), 0)], width,
                              functools.partial(_scaled_epilogue, factor=q_factor), [(BF16, F32)],
                              name="proj_q")
            m_p, m_s = _fox_attention(q_p, kb_p, vb_p, c_p, batch, q_s, k_s, v_s, lf_s, cache_k, cache_v,
                                      ck_pages, page_table, dec_seq)
            w_mix, lead, bk = w_o, (b,), 512
        x_p, x_s, h_p, h_s = _mm_res(m_p, m_s, w_mix, lead, x_p, x_s, g_post[l, 1], g_pre[l, 2:3], 1.0,
                                     bm=256 if l < n_a else 512, bk=bk, name="mixer_out")

        if l + 1 < depth:
            g_next = g_pre[l + 1, 0:1]
            if l + 1 == n_a:
                g_next = jnp.concatenate([g_next, g_kv.reshape(1, d_model)], axis=0)
            outs = ffn(x_p, x_s, h_p, h_s, l, 1, g_next)
            x_p, x_s, h_p, h_s = outs[:4]
            if l + 1 == n_a:
                hk_p, hk_s = outs[4:6]
        else:
            x_p, x_s = ffn(x_p, x_s, h_p, h_s, l, 1, None)

    k_p, k_s, v_p, v_s, lf_p, lf_s = kv_out
    return (x_p.reshape(batch, seq, d_model),
            x_s.reshape(dec_batch, dec_seq, d_model),
            k_p.reshape(batch, seq, n_heads, HEAD_DIM),
            v_p.reshape(batch, seq, n_heads, HEAD_DIM),
            lf_p[:, :n_heads].reshape(batch, seq, n_heads),
            k_s.reshape(dec_batch, dec_seq, n_heads, HEAD_DIM),
            v_s.reshape(dec_batch, dec_seq, n_heads, HEAD_DIM),
            lf_s[:, :n_heads].reshape(dec_batch, dec_seq, n_heads),
            jnp.stack(sgu_p),
            jnp.stack(sgu_s))
```

```python
import functools

import jax
import jax.numpy as jnp
from jax import lax
from jax.experimental import pallas as pl
from jax.experimental.pallas import tpu as pltpu

F32 = jnp.float32
BF16 = jnp.bfloat16

RMS_EPS = 1e-6
LN_EPS = 1e-5
NEG_INF = -1e30
LOG2E = 1.4426950408889634
CHUNK = 128
N_GROUPS = 16
HEAD_DIM = 128
PAGE = 128

V7X_VMEM_LIMIT_CAP = 56 * 1024 * 1024
LANES = 128


def _vmem_limit(estimate_bytes):
    return int(min(max(estimate_bytes * 5 // 4, 16 * 1024 * 1024), V7X_VMEM_LIMIT_CAP))


def _params(estimate_bytes, n_grid_dims):
    return pltpu.CompilerParams(
        dimension_semantics=("arbitrary",) * n_grid_dims,
        vmem_limit_bytes=_vmem_limit(estimate_bytes))


def _rms(x, g):
    return x * lax.rsqrt(jnp.mean(x * x, axis=-1, keepdims=True) + RMS_EPS) * g


def _dot(a, b):
    return jnp.dot(a, b, preferred_element_type=F32)


def _dot_exact(a, b):
    return jnp.dot(a, b, preferred_element_type=F32, precision=lax.Precision.HIGHEST)


def _weight_spec(w, lead, block, index):
    n_lead = len(lead)
    assert w.ndim == n_lead + 2
    return pl.BlockSpec((None,) * n_lead + block, lambda *g: tuple(lead) + index(*g))


def _prenorm_body(xp_ref, xs_ref, g_ref, hp_ref, hs_ref, *, n_tiles):
    i = pl.program_id(0)

    @pl.when(i < n_tiles)
    def _():
        hp_ref[...] = _rms(xp_ref[...], g_ref[...]).astype(BF16)

    @pl.when(i == n_tiles)
    def _():
        hs_ref[...] = _rms(xs_ref[...], g_ref[...]).astype(BF16)


def _prenorm(x_p, x_s, g, bm=512):
    mp, d = x_p.shape
    ms = x_s.shape[0]
    n_tiles = mp // bm
    last = n_tiles - 1
    row = lambda i: (jnp.minimum(i, last), 0)
    fixed = lambda i: (0, 0)
    return pl.pallas_call(
        functools.partial(_prenorm_body, n_tiles=n_tiles),
        grid=(n_tiles + 1,),
        in_specs=[pl.BlockSpec((bm, d), row), pl.BlockSpec((ms, d), fixed),
                  pl.BlockSpec((1, d), fixed)],
        out_specs=[pl.BlockSpec((bm, d), row), pl.BlockSpec((ms, d), fixed)],
        out_shape=[jax.ShapeDtypeStruct((mp, d), BF16), jax.ShapeDtypeStruct((ms, d), BF16)],
        compiler_params=_params(6 * bm * d * 4, 1),
        name="prenorm",
    )(x_p, x_s, g.reshape(1, d))


def _mm_ws_body(*refs, n_w, n_out, epilogue):
    hp_ref, hs_ref = refs[0], refs[1]
    w_refs = refs[2:2 + n_w]
    out_refs = refs[2 + n_w:2 + n_w + 2 * n_out]
    wbf_refs = refs[2 + n_w + 2 * n_out:]
    m = pl.program_id(1)

    def run(h_ref, o_refs):
        h = h_ref[...]
        outs = epilogue(*[_dot(h, wbf_ref[...]) for wbf_ref in wbf_refs])
        for o_ref, o in zip(o_refs, outs):
            o_ref[...] = o.astype(o_ref.dtype)

    @pl.when(m == 0)
    def _():
        for w_ref, wbf_ref in zip(w_refs, wbf_refs):
            wbf_ref[...] = w_ref[...].astype(BF16)
        run(hs_ref, out_refs[1::2])

    @pl.when(m > 0)
    def _():
        run(hp_ref, out_refs[0::2])


def _mm_ws(h_p, h_s, weights, n_cols, epilogue, out_dtypes, *, bm=1024, bn=512, name):
    mp, k = h_p.shape
    ms = h_s.shape[0]
    n_tiles = mp // bm
    n_w, n_out = len(weights), len(out_dtypes)
    prow = lambda m: jnp.maximum(m - 1, 0)
    in_specs = [pl.BlockSpec((bm, k), lambda n, m: (prow(m), 0)),
                pl.BlockSpec((ms, k), lambda n, m: (0, 0))]
    for w, lead, off in weights:
        assert off % bn == 0
        in_specs.append(_weight_spec(w, lead, (k, bn), lambda n, m, o=off // bn: (0, n + o)))
    out_specs, out_shape = [], []
    for dt_p, dt_s in out_dtypes:
        out_specs += [pl.BlockSpec((bm, bn), lambda n, m: (prow(m), n)),
                      pl.BlockSpec((ms, bn), lambda n, m: (0, n))]
        out_shape += [jax.ShapeDtypeStruct((mp, n_cols), dt_p),
                      jax.ShapeDtypeStruct((ms, n_cols), dt_s)]
    est = (2 * bm * k * 2 + n_w * k * bn * (2 * 4 + 2)
           + n_out * 2 * bm * bn * 4 + (n_w + n_out) * bm * bn * 4)
    return pl.pallas_call(
        functools.partial(_mm_ws_body, n_w=n_w, n_out=n_out, epilogue=epilogue),
        grid=(n_cols // bn, n_tiles + 1),
        in_specs=in_specs, out_specs=out_specs, out_shape=out_shape,
        scratch_shapes=[pltpu.VMEM((k, bn), BF16) for _ in weights],
        compiler_params=_params(est, 2),
        name=name,
    )(h_p, h_s, *[w for w, _, _ in weights])


def _swiglu_epilogue(g, u):
    return (g / (1.0 + jnp.exp(-g)) * u,)


def _gelu(z):
    return 0.5 * z * (1.0 + lax.erf(z * (2.0 ** -0.5)))


def _gelu2_epilogue(zu, zv):
    return (_gelu(zu), _gelu(zv))


def _kv_epilogue(zk, zv):
    return (zk, zk, zv, zv)


def _scaled_epilogue(z, *, factor):
    return (z * factor,)


def _mm_res_body(*refs, n_h, n_k, bk, n_tiles, coef):
    ap_ref, as_ref, w_ref, xp_ref, xs_ref, gpost_ref, gnext_ref = refs[:7]
    out_refs = refs[7:7 + 2 * (1 + n_h)]
    wbf_ref, y0_ref = refs[7 + 2 * (1 + n_h):]
    i = pl.program_id(0)

    @pl.when(i < n_k)
    def _():
        r0 = pl.multiple_of(i * bk, bk)
        w_chunk = w_ref[...].astype(BF16)
        wbf_ref[pl.ds(r0, bk), :] = w_chunk
        part = _dot(ap_ref[:, pl.ds(r0, bk)].astype(BF16), w_chunk)

        @pl.when(i == 0)
        def _():
            y0_ref[...] = part

        @pl.when(i > 0)
        def _():
            y0_ref[...] += part

    def run(a_ref, x_ref, o_refs, y=None):
        if y is None:
            y = _dot(a_ref[...].astype(BF16), wbf_ref[...])
        x_new = x_ref[...] + coef * _rms(y, gpost_ref[...])
        o_refs[0][...] = x_new
        if n_h:
            xn = x_new * lax.rsqrt(jnp.mean(x_new * x_new, axis=-1, keepdims=True) + RMS_EPS)
            for j in range(n_h):
                o_refs[1 + j][...] = (xn * gnext_ref[j:j + 1, :]).astype(BF16)

    @pl.when(i == n_k)
    def _():
        run(ap_ref, xp_ref, out_refs[0::2], y0_ref[...])

    @pl.when(jnp.logical_and(i > n_k, i < n_k + n_tiles))
    def _():
        run(ap_ref, xp_ref, out_refs[0::2])

    @pl.when(i == n_k + n_tiles)
    def _():
        run(as_ref, xs_ref, out_refs[1::2])


def _mm_res(a_p, a_s, w, lead, x_p, x_s, g_post, g_next, coef, *, bm, bk, name):
    mp, k = a_p.shape
    ms = a_s.shape[0]
    d = w.shape[-1]
    n_h = 0 if g_next is None else g_next.shape[0]
    g_next_arr = jnp.zeros((1, d), F32) if g_next is None else g_next
    n_tiles = mp // bm
    n_k = k // bk
    assert n_k * bk == k
    row = lambda i: (jnp.clip(i - n_k, 0, n_tiles - 1), 0)
    fixed = lambda i: (0, 0)
    in_specs = [pl.BlockSpec((bm, k), row), pl.BlockSpec((ms, k), fixed),
                _weight_spec(w, lead, (bk, d), lambda i: (jnp.minimum(i, n_k - 1), 0)),
                pl.BlockSpec((bm, d), row), pl.BlockSpec((ms, d), fixed),
                pl.BlockSpec((1, d), fixed), pl.BlockSpec(g_next_arr.shape, fixed)]
    out_specs = [pl.BlockSpec((bm, d), row), pl.BlockSpec((ms, d), fixed)]
    out_shape = [jax.ShapeDtypeStruct((mp, d), F32), jax.ShapeDtypeStruct((ms, d), F32)]
    for _ in range(n_h):
        out_specs += [pl.BlockSpec((bm, d), row), pl.BlockSpec((ms, d), fixed)]
        out_shape += [jax.ShapeDtypeStruct((mp, d), BF16), jax.ShapeDtypeStruct((ms, d), BF16)]
    est = (k * d * 2 + 2 * bk * d * 4 + 2 * bm * k * a_p.dtype.itemsize + 4 * bm * d * 4
           + n_h * 2 * bm * d * 2 + 3 * bm * d * 4)
    return pl.pallas_call(
        functools.partial(_mm_res_body, n_h=n_h, n_k=n_k, bk=bk, n_tiles=n_tiles, coef=coef),
        grid=(n_k + n_tiles + 1,),
        in_specs=in_specs, out_specs=out_specs, out_shape=out_shape,
        scratch_shapes=[pltpu.VMEM((k, d), BF16), pltpu.VMEM((bm, d), F32)],
        compiler_params=_params(est, 1),
        name=name,
    )(a_p, a_s, w, x_p, x_s, g_post.reshape(1, d), g_next_arr)


def _sgu_body(up_ref, vp_ref, us_ref, vs_ref, lng_ref, lnb_ref, wp_ref, bp_ref, ws_ref, bs_ref,
              pp_ref, ps_ref, vtp_ref, vts_ref, *, n_chunks):
    c = pl.program_id(0)

    def run(u_ref, v_ref, w_ref, b_ref, p_ref, vt_ref):
        rows = v_ref.shape[0]
        gw = v_ref.shape[1] // N_GROUPS
        v = v_ref[...]
        xc = v - jnp.mean(v, axis=-1, keepdims=True)
        var = jnp.mean(xc * xc, axis=-1, keepdims=True)
        vn = xc * lax.rsqrt(var + LN_EPS) * lng_ref[...] + lnb_ref[...]
        vt_ref[...] = vn
        vb = vn.astype(BF16)
        causal = (lax.broadcasted_iota(jnp.int32, (rows, rows), 0)
                  >= lax.broadcasted_iota(jnp.int32, (rows, rows), 1))
        for g in range(N_GROUPS):
            wm = jnp.where(causal, w_ref[g], 0.0).astype(BF16)
            mixed = _dot(wm, vb[:, g * gw:(g + 1) * gw]) + b_ref[:, g:g + 1]
            p_ref[:, g * gw:(g + 1) * gw] = (
                u_ref[:, g * gw:(g + 1) * gw].astype(F32) * mixed).astype(BF16)

    @pl.when(c < n_chunks)
    def _():
        run(up_ref, vp_ref, wp_ref, bp_ref, pp_ref, vtp_ref)

    @pl.when(c == n_chunks)
    def _():
        run(us_ref, vs_ref, ws_ref, bs_ref, ps_ref, vts_ref)


def _sgu_mix(u_p, v_p, u_s, v_s, ln_g, ln_b, w_sp, b_sp, batch, dec_batch):
    mp, du = u_p.shape
    ms = u_s.shape[0]
    n_chunks = mp // CHUNK
    chunks_per_seq = n_chunks // batch
    dec_seq = ms // dec_batch
    last = n_chunks - 1
    eye = jnp.eye(dec_batch, dtype=F32)
    w_s = (eye[None, :, None, :, None] * w_sp[:, None, :dec_seq, None, :dec_seq]
           ).reshape(N_GROUPS, ms, ms)
    b_s = jnp.tile(jnp.transpose(b_sp[:, :dec_seq]), (dec_batch, 1))
    b_p = jnp.transpose(b_sp)
    row = lambda c: (jnp.minimum(c, last), 0)
    fixed2 = lambda c: (0, 0)
    fixed3 = lambda c: (0, 0, 0)
    est = 2 * CHUNK * du * (2 + 4 + 2 + 4) + 8 * CHUNK * du * 4
    return pl.pallas_call(
        functools.partial(_sgu_body, n_chunks=n_chunks),
        grid=(n_chunks + 1,),
        in_specs=[pl.BlockSpec((CHUNK, du), row), pl.BlockSpec((CHUNK, du), row),
                  pl.BlockSpec((ms, du), fixed2), pl.BlockSpec((ms, du), fixed2),
                  pl.BlockSpec((1, du), fixed2), pl.BlockSpec((1, du), fixed2),
                  pl.BlockSpec((N_GROUPS, CHUNK, CHUNK), fixed3), pl.BlockSpec((CHUNK, N_GROUPS), fixed2),
                  pl.BlockSpec((N_GROUPS, ms, ms), fixed3), pl.BlockSpec((ms, N_GROUPS), fixed2)],
        out_specs=[pl.BlockSpec((CHUNK, du), row), pl.BlockSpec((ms, du), fixed2),
                   pl.BlockSpec((CHUNK, du), lambda c: (jnp.minimum(c, last) // chunks_per_seq, 0)),
                   pl.BlockSpec((ms, du), fixed2)],
        out_shape=[jax.ShapeDtypeStruct((mp, du), BF16), jax.ShapeDtypeStruct((ms, du), BF16),
                   jax.ShapeDtypeStruct((batch * CHUNK, du), F32), jax.ShapeDtypeStruct((ms, du), F32)],
        compiler_params=_params(est, 1),
        name="sgu_mix",
    )(u_p, v_p, u_s, v_s, ln_g.reshape(1, du), ln_b.reshape(1, du), w_sp, b_p, w_s, b_s)


def _logf_body(xp_ref, xs_ref, g_ref, w_ref, b_ref, lp_ref, ls_ref, *, n_tiles):
    i = pl.program_id(0)

    def run(x_ref, o_ref):
        z = _dot_exact(_rms(x_ref[...], g_ref[...]), w_ref[...]) + b_ref[...]
        o_ref[...] = jnp.minimum(z, 0.0) - jnp.log1p(jnp.exp(-jnp.abs(z)))

    @pl.when(i < n_tiles)
    def _():
        run(xp_ref, lp_ref)

    @pl.when(i == n_tiles)
    def _():
        run(xs_ref, ls_ref)


def _logf(x_p, x_s, g_kv, w_f, b_f, bm=512):
    mp, d = x_p.shape
    ms = x_s.shape[0]
    nh = w_f.shape[1]
    w_pad = jnp.pad(w_f, ((0, 0), (0, LANES - nh)))
    b_pad = jnp.pad(b_f, (0, LANES - nh)).reshape(1, LANES)
    n_tiles = mp // bm
    last = n_tiles - 1
    row = lambda i: (jnp.minimum(i, last), 0)
    fixed = lambda i: (0, 0)
    return pl.pallas_call(
        functools.partial(_logf_body, n_tiles=n_tiles),
        grid=(n_tiles + 1,),
        in_specs=[pl.BlockSpec((bm, d), row), pl.BlockSpec((ms, d), fixed),
                  pl.BlockSpec((1, d), fixed), pl.BlockSpec((d, LANES), fixed),
                  pl.BlockSpec((1, LANES), fixed)],
        out_specs=[pl.BlockSpec((bm, LANES), row), pl.BlockSpec((ms, LANES), fixed)],
        out_shape=[jax.ShapeDtypeStruct((mp, LANES), F32), jax.ShapeDtypeStruct((ms, LANES), F32)],
        compiler_params=_params(8 * bm * d * 4, 1),
        name="logf",
    )(x_p, x_s, g_kv.reshape(1, d), w_pad, b_pad)


def _cumsum_body(l_ref, c_ref, *, n_chunks):
    tril = (lax.broadcasted_iota(jnp.int32, (CHUNK, CHUNK), 0)
            >= lax.broadcasted_iota(jnp.int32, (CHUNK, CHUNK), 1)).astype(F32)

    def step(j, carry):
        r0 = pl.multiple_of(j * CHUNK, CHUNK)
        local = _dot_exact(tril, l_ref[pl.ds(r0, CHUNK), :]) + carry
        c_ref[pl.ds(r0, CHUNK), :] = local
        return local[CHUNK - 1:CHUNK, :]

    lax.fori_loop(0, n_chunks, step, jnp.zeros((1, LANES), F32))


def _cumsum_rows(lf_p, batch):
    mp = lf_p.shape[0]
    seq = mp // batch
    return pl.pallas_call(
        functools.partial(_cumsum_body, n_chunks=seq // CHUNK),
        grid=(batch,),
        in_specs=[pl.BlockSpec((seq, LANES), lambda b: (b, 0))],
        out_specs=pl.BlockSpec((seq, LANES), lambda b: (b, 0)),
        out_shape=jax.ShapeDtypeStruct((mp, LANES), F32),
        compiler_params=_params(4 * seq * LANES * 4, 1),
        name="cumsum_logf",
    )(lf_p)


def _head_column(c, h):
    lane = lax.broadcasted_iota(jnp.int32, (1, LANES), 1)
    return jnp.sum(jnp.where(lane == h, c, 0.0), axis=-1, keepdims=True) * LOG2E


def _split3(c):
    c1 = c.astype(BF16).astype(F32)
    r1 = c - c1
    c2 = r1.astype(BF16).astype(F32)
    c3 = (r1 - c2).astype(BF16).astype(F32)
    return c1, c2, c3


def _bias_columns(c_col, query_side):
    rows = c_col.shape[0]
    lane = lax.broadcasted_iota(jnp.int32, (rows, LANES), 1)
    c1, c2, c3 = _split3(c_col)
    if query_side:
        aug = jnp.where(lane == 0, c1, jnp.where(lane == 1, c2, jnp.where(lane == 2, c3, jnp.where(lane < 6, 1.0, 0.0))))
    else:
        aug = jnp.where(lane < 3, 1.0, jnp.where(lane == 3, -c1, jnp.where(lane == 4, -c2, jnp.where(lane == 5, -c3, 0.0))))
    return aug.astype(BF16)


VT_ROWS = HEAD_DIM + 16


def _fox_prompt_body(q_ref, k_ref, v_ref, cq_ref, ck_ref, o_ref, kaug_ref, vt_ref, *, tq, tk, hs, nq, sum_on_mxu):
    hg = pl.program_id(1)
    qi = pl.program_id(2)
    head_cols = [slice(a * HEAD_DIM, (a + 1) * HEAD_DIM) for a in range(hs)]
    aug_cols = [slice(a * 2 * HEAD_DIM, (a + 1) * 2 * HEAD_DIM) for a in range(hs)]

    @pl.when(qi == 0)
    def _():
        ck = ck_ref[...]
        seq = k_ref.shape[0]
        ones_row = (lax.broadcasted_iota(jnp.int32, (VT_ROWS - HEAD_DIM, seq), 0) == 0).astype(BF16)
        for a in range(hs):
            kaug_ref[:, aug_cols[a]] = jnp.concatenate(
                [k_ref[:, head_cols[a]], _bias_columns(_head_column(ck, hg * hs + a), False)], axis=1)
            v_t = v_ref[:, head_cols[a]].astype(F32).T.astype(BF16)
            vt_ref[a] = jnp.concatenate([v_t, ones_row], axis=0) if sum_on_mxu else v_t

    cq = cq_ref[...]
    q_t = [jnp.concatenate([q_ref[:, head_cols[a]].astype(F32),
                            _bias_columns(_head_column(cq, hg * hs + a), True).astype(F32)],
                           axis=1).T.astype(BF16) for a in range(hs)]

    def tile(j, carry, masked):
        r0 = j * tk
        out = []
        for a in range(hs):
            m, l, acc = carry[a]
            s = _dot(kaug_ref[pl.ds(r0, tk), aug_cols[a]], q_t[a])
            if masked:
                s = jnp.where(lax.broadcasted_iota(jnp.int32, (tk, tq), 0)
                              <= lax.broadcasted_iota(jnp.int32, (tk, tq), 1), s, NEG_INF)
            m_new = jnp.maximum(m, jnp.max(s, axis=0, keepdims=True))
            alpha = jnp.exp2(m - m_new)
            p = jnp.exp2(s - m_new)
            if not sum_on_mxu:
                l = alpha * l + jnp.sum(p, axis=0, keepdims=True)
            acc = alpha * acc + _dot(vt_ref[a, :, pl.ds(r0, tk)], p.astype(BF16))
            out.append((m_new, l, acc))
        return tuple(out)

    for n_full in range(nq):
        @pl.when(qi == n_full)
        def _(n_full=n_full):
            carry = tuple((jnp.full((1, tq), NEG_INF, F32), jnp.zeros((1, tq), F32),
                           jnp.zeros((vt_ref.shape[1], tq), F32)) for _ in range(hs))
            for j in range(n_full):
                carry = tile(j, carry, False)
            carry = tile(n_full, carry, True)
            for a in range(hs):
                _, l, acc = carry[a]
                if sum_on_mxu:
                    l = acc[HEAD_DIM:HEAD_DIM + 1]
                o_ref[:, head_cols[a]] = (acc[:HEAD_DIM] / l).T.astype(o_ref.dtype)


def _fox_prompt(q_p, k_bf, v_bf, c_p, batch, tq=1024, tk=1024, hs=1, sum_on_mxu=False):
    assert tq == tk
    mp, width = q_p.shape
    n_heads = width // HEAD_DIM
    seq = mp // batch
    nq = seq // tq
    hw = hs * HEAD_DIM
    est = 2 * (2 * seq * hw * 2) + 3 * seq * LANES * 4 + 3 * seq * hw * 2 + hs * 6 * tq * tk * 4
    return pl.pallas_call(
        functools.partial(_fox_prompt_body, tq=tq, tk=tk, hs=hs, nq=nq, sum_on_mxu=sum_on_mxu),
        grid=(batch, n_heads // hs, nq),
        in_specs=[pl.BlockSpec((tq, hw), lambda b, h, i: (b * nq + i, h)),
                  pl.BlockSpec((seq, hw), lambda b, h, i: (b, h)),
                  pl.BlockSpec((seq, hw), lambda b, h, i: (b, h)),
                  pl.BlockSpec((tq, LANES), lambda b, h, i: (b * nq + i, 0)),
                  pl.BlockSpec((seq, LANES), lambda b, h, i: (b, 0))],
        out_specs=pl.BlockSpec((tq, hw), lambda b, h, i: (b * nq + i, h)),
        out_shape=jax.ShapeDtypeStruct((mp, width), BF16),
        scratch_shapes=[pltpu.VMEM((seq, 2 * hw), BF16),
                        pltpu.VMEM((hs, VT_ROWS if sum_on_mxu else HEAD_DIM, seq), BF16)],
        compiler_params=_params(est, 3),
        name="fox_prompt",
    )(q_p, k_bf, v_bf, c_p, c_p)


def _ck_past_body(pt_ref, *refs, n_group, n_heads, dec_seq):
    lf_refs = refs[:n_group]
    out_ref, carry_ref = refs[n_group:]
    p = pl.program_id(1)

    @pl.when(p == 0)
    def _():
        carry_ref[...] = jnp.zeros_like(carry_ref)

    expand = (lax.broadcasted_iota(jnp.int32, (n_heads, LANES), 1) // dec_seq
              == lax.broadcasted_iota(jnp.int32, (n_heads, LANES), 0)).astype(BF16)
    later = (lax.broadcasted_iota(jnp.int32, (PAGE, PAGE), 1)
             > lax.broadcasted_iota(jnp.int32, (PAGE, PAGE), 0)).astype(BF16)
    lf = jnp.concatenate([lf_refs[g][0] for g in range(n_group)], axis=0)
    lf_e, within = None, None
    for term in _split3(lf):
        e = _dot(term.astype(BF16), expand)
        e_by_lanes = jnp.concatenate([e[g * PAGE:(g + 1) * PAGE] for g in range(n_group)], axis=1)
        w = _dot(later, e_by_lanes.astype(BF16))
        lf_e = e if lf_e is None else lf_e + e
        within = w if within is None else within + w
    carry = carry_ref[...]
    for g in range(n_group):
        within_g = within[:, g * LANES:(g + 1) * LANES]
        out_ref[0, n_group - 1 - g] = -(within_g + carry)
        carry = carry + within_g[0:1, :] + lf_e[g * PAGE:g * PAGE + 1, :]
    carry_ref[...] = carry


def _ck_past(cache_logf, page_table, dec_seq, n_group=16):
    n_seq, n_pages = page_table.shape
    n_heads = cache_logf.shape[2]
    assert n_heads * dec_seq == LANES and n_pages % n_group == 0
    lf_map = lambda b, p, pt, g: (pt[b, n_pages - 1 - (p * n_group + g)], 0, 0)
    n_steps = n_pages // n_group
    grid_spec = pltpu.PrefetchScalarGridSpec(
        num_scalar_prefetch=1,
        grid=(n_seq, n_steps),
        in_specs=[pl.BlockSpec((1, PAGE, n_heads), functools.partial(lf_map, g=g))
                  for g in range(n_group)],
        out_specs=pl.BlockSpec((1, n_group, PAGE, LANES), lambda b, p, pt: (b, n_steps - 1 - p, 0, 0)),
        scratch_shapes=[pltpu.VMEM((1, LANES), F32)])
    return pl.pallas_call(
        functools.partial(_ck_past_body, n_group=n_group, n_heads=n_heads, dec_seq=dec_seq),
        grid_spec=grid_spec,
        out_shape=jax.ShapeDtypeStruct((n_seq, n_pages, PAGE, LANES), F32),
        compiler_params=_params(64 * PAGE * LANES * 4, 2),
        name="ck_past",
    )(page_table, *([cache_logf] * n_group))


def _fox_sample_body(pt_ref, *refs, n_group, n_heads, dec_seq):
    k_refs = refs[:n_group]
    v_refs = refs[n_group:2 * n_group]
    ck_ref, qt_ref, kn_ref, vn_ref, lfn_ref, o_ref, m_ref, l_ref, acc_ref, cq_ref = refs[2 * n_group:]
    p = pl.program_id(1)
    n_steps = pl.num_programs(1)
    qt = qt_ref[0]
    eye = (lax.broadcasted_iota(jnp.int32, (LANES, LANES), 0)
           == lax.broadcasted_iota(jnp.int32, (LANES, LANES), 1))
    head_match = (lax.broadcasted_iota(jnp.int32, (n_heads, LANES), 1) // dec_seq
                  == lax.broadcasted_iota(jnp.int32, (n_heads, LANES), 0))

    def to_column(row):
        return jnp.sum(jnp.where(eye, row, 0.0), axis=-1, keepdims=True)

    def scores(rows_ref, bias, keep):
        s = _dot(rows_ref[0].astype(BF16), qt)
        s = s.reshape(bias.shape[0], n_heads, LANES) + bias[:, None, :]
        return jnp.where(keep, s, NEG_INF)

    def col_max(s3):
        return jnp.max(jnp.max(s3, axis=0), axis=0, keepdims=True)

    def col_sum(p3):
        return jnp.sum(jnp.sum(p3, axis=0), axis=0, keepdims=True)

    def weighted_values(p3, rows_ref):
        pr = p3.reshape(p3.shape[0] * n_heads, LANES).astype(BF16)
        return lax.dot_general(pr, rows_ref[0].astype(BF16), (((0,), (0,)), ((), ())),
                               preferred_element_type=F32)

    @pl.when(p == 0)
    def _():
        expand = (lax.broadcasted_iota(jnp.int32, (LANES, LANES), 1) // dec_seq
                  == lax.broadcasted_iota(jnp.int32, (LANES, LANES), 0)).astype(F32)
        lf_e = _dot_exact(lfn_ref[...], expand)
        rows = [lf_e[0:1, :]]
        for j in range(1, dec_seq):
            rows.append(rows[-1] + lf_e[j:j + 1, :])
        c_new = jnp.concatenate(rows, axis=0)
        key_j = lax.broadcasted_iota(jnp.int32, (dec_seq, LANES), 0)
        col_t = lax.broadcasted_iota(jnp.int32, (dec_seq, LANES), 1) % dec_seq
        cq = jnp.sum(jnp.where(key_j == col_t, c_new, 0.0), axis=0, keepdims=True)
        cq_ref[...] = cq
        causal = (key_j <= col_t)[:, None, :]
        s3 = scores(kn_ref, (cq - c_new) * LOG2E, jnp.logical_and(causal, head_match[None]))
        m = col_max(s3)
        p3 = jnp.exp2(s3 - m[None])
        m_ref[...] = m
        l_ref[...] = col_sum(p3)
        acc_ref[...] = weighted_values(p3, vn_ref)

    cq = cq_ref[...]
    s_pages = [scores(k_refs[g], (cq - ck_ref[0, n_group - 1 - g]) * LOG2E, head_match[None])
               for g in range(n_group)]
    m_old = m_ref[...]
    m_new = m_old
    for s3 in s_pages:
        m_new = jnp.maximum(m_new, col_max(s3))
    alpha = jnp.exp2(m_old - m_new)
    l_new = alpha * l_ref[...]
    pv = None
    for g in range(n_group):
        p3 = jnp.exp2(s_pages[g] - m_new[None])
        l_new = l_new + col_sum(p3)
        t = weighted_values(p3, v_refs[g])
        pv = t if pv is None else pv + t
    m_ref[...] = m_new
    l_ref[...] = l_new
    acc_ref[...] = to_column(alpha) * acc_ref[...] + pv

    @pl.when(p == n_steps - 1)
    def _():
        out = acc_ref[...] / to_column(l_ref[...])
        for h in range(n_heads):
            o_ref[:, h * HEAD_DIM:(h + 1) * HEAD_DIM] = out[h * dec_seq:(h + 1) * dec_seq, :]


def _fox_sample(q_s, k_s, v_s, lf_s, cache_k, cache_v, ck_pages, page_table, dec_seq, n_group=8):
    n_seq, n_pages = page_table.shape
    n_heads = cache_k.shape[2]
    width = n_heads * HEAD_DIM
    page_rows = PAGE * n_heads
    new_rows = dec_seq * n_heads
    n_steps = n_pages // n_group
    ck_rows = cache_k.reshape(cache_k.shape[0], page_rows, HEAD_DIM)
    cv_rows = cache_v.reshape(cache_v.shape[0], page_rows, HEAD_DIM)
    kn_rows = k_s.reshape(n_seq, new_rows, HEAD_DIM)
    vn_rows = v_s.reshape(n_seq, new_rows, HEAD_DIM)
    qt = jnp.transpose(q_s.reshape(n_seq, dec_seq, n_heads, HEAD_DIM), (0, 3, 2, 1))
    qt = qt.reshape(n_seq, HEAD_DIM, n_heads * dec_seq).astype(BF16)
    page_map = lambda b, p, pt, g: (pt[b, n_pages - 1 - (p * n_group + g)], 0, 0)
    seq3 = lambda b, p, pt: (b, 0, 0)
    page_block = (1, page_rows, HEAD_DIM)
    in_specs = ([pl.BlockSpec(page_block, functools.partial(page_map, g=g)) for g in range(n_group)]
                + [pl.BlockSpec(page_block, functools.partial(page_map, g=g)) for g in range(n_group)]
                + [pl.BlockSpec((1, n_group, PAGE, LANES), lambda b, p, pt: (b, n_steps - 1 - p, 0, 0)),
                   pl.BlockSpec((1, HEAD_DIM, LANES), seq3),
                   pl.BlockSpec((1, new_rows, HEAD_DIM), seq3), pl.BlockSpec((1, new_rows, HEAD_DIM), seq3),
                   pl.BlockSpec((dec_seq, LANES), lambda b, p, pt: (b, 0))])
    grid_spec = pltpu.PrefetchScalarGridSpec(
        num_scalar_prefetch=1,
        grid=(n_seq, n_steps),
        in_specs=in_specs,
        out_specs=pl.BlockSpec((dec_seq, width), lambda b, p, pt: (b, 0)),
        scratch_shapes=[pltpu.VMEM((1, LANES), F32), pltpu.VMEM((1, LANES), F32),
                        pltpu.VMEM((LANES, HEAD_DIM), F32), pltpu.VMEM((1, LANES), F32)])
    est = 2 * 2 * n_group * page_rows * HEAD_DIM * 4 + n_group * page_rows * LANES * 12
    return pl.pallas_call(
        functools.partial(_fox_sample_body, n_group=n_group, n_heads=n_heads, dec_seq=dec_seq),
        grid_spec=grid_spec,
        out_shape=jax.ShapeDtypeStruct((n_seq * dec_seq, width), F32),
        compiler_params=_params(est, 2),
        name="fox_sample",
    )(page_table, *([ck_rows] * n_group), *([cv_rows] * n_group), ck_pages, qt, kn_rows, vn_rows, lf_s)


def kernel(x_prompt, x_sample, cache_k, cache_v, cache_logf, page_table, g_pre, g_post, w_ffn_gate, w_ffn_up, w_ffn_down, w_a_in, a_ln_g, a_ln_b, w_a_spatial, b_a_spatial, w_a_out, g_kv, w_kvf, b_f, w_q, w_o):
    batch, seq, d_model = x_prompt.shape
    dec_batch, dec_seq, _ = x_sample.shape
    depth = g_pre.shape[0]
    n_a = w_a_in.shape[0]
    d_ff = w_ffn_gate.shape[-1]
    d_u = a_ln_g.shape[-1]
    n_heads = b_f.shape[0]
    width = n_heads * HEAD_DIM
    q_factor = LOG2E * HEAD_DIM ** -0.5

    x_p = x_prompt.reshape(batch * seq, d_model)
    x_s = x_sample.reshape(dec_batch * dec_seq, d_model)
    h_p, h_s = _prenorm(x_p, x_s, g_pre[0, 0])

    def ffn(x_p, x_s, h_p, h_s, l, j, g_next):
        a_p, a_s = _mm_ws(h_p, h_s, [(w_ffn_gate, (l, j), 0), (w_ffn_up, (l, j), 0)], d_ff,
                          _swiglu_epilogue, [(BF16, BF16)], bm=2048 if j else 1024, name="ffn_gate_up")
        return _mm_res(a_p, a_s, w_ffn_down, (l, j), x_p, x_s, g_post[l, 2 * j], g_next,
                       0.5, bm=256, bk=512, name="ffn_down")

    sgu_p, sgu_s = [], []
    kv_out = None
    for l in range(depth):
        if l == n_a:
            k_p, k_s, kb_p, _, v_p, v_s, vb_p, _ = _mm_ws(
                hk_p, hk_s, [(w_kvf, (), 0), (w_kvf, (), width)], width, _kv_epilogue,
                [(F32, F32), (BF16, BF16), (F32, F32), (BF16, BF16)], name="proj_kv")
            lf_p, lf_s = _logf(x_p, x_s, g_kv, w_kvf[:, 2 * width:], b_f)
            c_p = _cumsum_rows(lf_p, batch)
            ck_pages = _ck_past(cache_logf, page_table, dec_seq)
            kv_out = (k_p, k_s, v_p, v_s, lf_p, lf_s)

        x_p, x_s, h_p, h_s = ffn(x_p, x_s, h_p, h_s, l, 0, g_pre[l, 1:2])

        if l < n_a:
            u_p, u_s, v_p_raw, v_s_raw = _mm_ws(
                h_p, h_s, [(w_a_in, (l,), 0), (w_a_in, (l,), d_u)], d_u, _gelu2_epilogue,
                [(BF16, BF16), (F32, F32)], name="sgu_in")
            m_p, m_s, vt_p, vt_s = _sgu_mix(u_p, v_p_raw, u_s, v_s_raw, a_ln_g[l], a_ln_b[l],
                                            w_a_spatial[l], b_a_spatial[l], batch, dec_batch)
            sgu_p.append(vt_p.reshape(batch, CHUNK, d_u))
            sgu_s.append(vt_s.reshape(dec_batch, dec_seq, d_u))
            w_mix, lead, bk = w_a_out, (l,), 512
        else:
            b = l - n_a
            q_p, q_s = _mm_ws(h_p, h_s, [(w_q, (b,), 0)], width,
                              functools.partial(_scaled_epilogue, factor=q_factor), [(BF16, F32)],
                              bn=1024 if b else 512, name="proj_q")
            m_p = _fox_prompt(q_p, kb_p, vb_p, c_p, batch, hs=2 if b else 1, sum_on_mxu=not b)
            m_s = _fox_sample(q_s, k_s, v_s, lf_s, cache_k, cache_v, ck_pages, page_table, dec_seq)
            w_mix, lead, bk = w_o, (b,), 512
        x_p, x_s, h_p, h_s = _mm_res(m_p, m_s, w_mix, lead, x_p, x_s, g_post[l, 1], g_pre[l, 2:3], 1.0,
                                     bm=256 if l < n_a else 512, bk=bk, name="mixer_out")

        if l + 1 < depth:
            g_next = g_pre[l + 1, 0:1]
            if l + 1 == n_a:
                g_next = jnp.concatenate([g_next, g_kv.reshape(1, d_model)], axis=0)
            outs = ffn(x_p, x_s, h_p, h_s, l, 1, g_next)
            x_p, x_s, h_p, h_s = outs[:4]
            if l + 1 == n_a:
                hk_p, hk_s = outs[4:6]
        else:
            x_p, x_s = ffn(x_p, x_s, h_p, h_s, l, 1, None)

    k_p, k_s, v_p, v_s, lf_p, lf_s = kv_out
    return (x_p.reshape(batch, seq, d_model),
            x_s.reshape(dec_batch, dec_seq, d_model),
            k_p.reshape(batch, seq, n_heads, HEAD_DIM),
            v_p.reshape(batch, seq, n_heads, HEAD_DIM),
            lf_p[:, :n_heads].reshape(batch, seq, n_heads),
            k_s.reshape(dec_batch, dec_seq, n_heads, HEAD_DIM),
            v_s.reshape(dec_batch, dec_seq, n_heads, HEAD_DIM),
            lf_s[:, :n_heads].reshape(dec_batch, dec_seq, n_heads),
            jnp.stack(sgu_p),
            jnp.stack(sgu_s))
```

```python
import functools

import jax
import jax.numpy as jnp
from jax import lax
from jax.experimental import pallas as pl
from jax.experimental.pallas import tpu as pltpu

F32 = jnp.float32
BF16 = jnp.bfloat16

RMS_EPS = 1e-6
LN_EPS = 1e-5
NEG_INF = -1e30
LOG2E = 1.4426950408889634
CHUNK = 128
N_GROUPS = 16
HEAD_DIM = 128
PAGE = 128

V7X_VMEM_LIMIT_CAP = 56 * 1024 * 1024
LANES = 128


def _vmem_limit(estimate_bytes):
    return int(min(max(estimate_bytes * 5 // 4, 16 * 1024 * 1024), V7X_VMEM_LIMIT_CAP))


def _params(estimate_bytes, n_grid_dims):
    return pltpu.CompilerParams(
        dimension_semantics=("arbitrary",) * n_grid_dims,
        vmem_limit_bytes=_vmem_limit(estimate_bytes))


def _rms(x, g):
    return x * lax.rsqrt(jnp.mean(x * x, axis=-1, keepdims=True) + RMS_EPS) * g


def _dot(a, b):
    return jnp.dot(a, b, preferred_element_type=F32)


def _dot_exact(a, b):
    return jnp.dot(a, b, preferred_element_type=F32, precision=lax.Precision.HIGHEST)


def _weight_spec(w, lead, block, index):
    n_lead = len(lead)
    assert w.ndim == n_lead + 2
    return pl.BlockSpec((None,) * n_lead + block, lambda *g: tuple(lead) + index(*g))


def _prenorm_body(xp_ref, xs_ref, g_ref, hp_ref, hs_ref, *, n_tiles):
    i = pl.program_id(0)

    @pl.when(i < n_tiles)
    def _():
        hp_ref[...] = _rms(xp_ref[...], g_ref[...]).astype(BF16)

    @pl.when(i == n_tiles)
    def _():
        hs_ref[...] = _rms(xs_ref[...], g_ref[...]).astype(BF16)


def _prenorm(x_p, x_s, g, bm=512):
    mp, d = x_p.shape
    ms = x_s.shape[0]
    n_tiles = mp // bm
    last = n_tiles - 1
    row = lambda i: (jnp.minimum(i, last), 0)
    fixed = lambda i: (0, 0)
    return pl.pallas_call(
        functools.partial(_prenorm_body, n_tiles=n_tiles),
        grid=(n_tiles + 1,),
        in_specs=[pl.BlockSpec((bm, d), row), pl.BlockSpec((ms, d), fixed),
                  pl.BlockSpec((1, d), fixed)],
        out_specs=[pl.BlockSpec((bm, d), row), pl.BlockSpec((ms, d), fixed)],
        out_shape=[jax.ShapeDtypeStruct((mp, d), BF16), jax.ShapeDtypeStruct((ms, d), BF16)],
        compiler_params=_params(6 * bm * d * 4, 1),
        name="prenorm",
    )(x_p, x_s, g.reshape(1, d))


def _mm_ws_body(*refs, n_w, n_out, epilogue):
    hp_ref, hs_ref = refs[0], refs[1]
    w_refs = refs[2:2 + n_w]
    out_refs = refs[2 + n_w:2 + n_w + 2 * n_out]
    wbf_refs = refs[2 + n_w + 2 * n_out:]
    m = pl.program_id(1)

    def run(h_ref, o_refs):
        h = h_ref[...]
        outs = epilogue(*[_dot(h, wbf_ref[...]) for wbf_ref in wbf_refs])
        for o_ref, o in zip(o_refs, outs):
            o_ref[...] = o.astype(o_ref.dtype)

    @pl.when(m == 0)
    def _():
        for w_ref, wbf_ref in zip(w_refs, wbf_refs):
            wbf_ref[...] = w_ref[...].astype(BF16)
        run(hs_ref, out_refs[1::2])

    @pl.when(m > 0)
    def _():
        run(hp_ref, out_refs[0::2])


def _mm_ws(h_p, h_s, weights, n_cols, epilogue, out_dtypes, *, bm=1024, bn=512, name):
    mp, k = h_p.shape
    ms = h_s.shape[0]
    n_tiles = mp // bm
    n_w, n_out = len(weights), len(out_dtypes)
    prow = lambda m: jnp.maximum(m - 1, 0)
    in_specs = [pl.BlockSpec((bm, k), lambda n, m: (prow(m), 0)),
                pl.BlockSpec((ms, k), lambda n, m: (0, 0))]
    for w, lead, off in weights:
        assert off % bn == 0
        in_specs.append(_weight_spec(w, lead, (k, bn), lambda n, m, o=off // bn: (0, n + o)))
    out_specs, out_shape = [], []
    for dt_p, dt_s in out_dtypes:
        out_specs += [pl.BlockSpec((bm, bn), lambda n, m: (prow(m), n)),
                      pl.BlockSpec((ms, bn), lambda n, m: (0, n))]
        out_shape += [jax.ShapeDtypeStruct((mp, n_cols), dt_p),
                      jax.ShapeDtypeStruct((ms, n_cols), dt_s)]
    est = (2 * bm * k * 2 + n_w * k * bn * (2 * 4 + 2)
           + n_out * 2 * bm * bn * 4 + (n_w + n_out) * bm * bn * 4)
    return pl.pallas_call(
        functools.partial(_mm_ws_body, n_w=n_w, n_out=n_out, epilogue=epilogue),
        grid=(n_cols // bn, n_tiles + 1),
        in_specs=in_specs, out_specs=out_specs, out_shape=out_shape,
        scratch_shapes=[pltpu.VMEM((k, bn), BF16) for _ in weights],
        compiler_params=_params(est, 2),
        name=name,
    )(h_p, h_s, *[w for w, _, _ in weights])


def _swiglu_epilogue(g, u):
    return (g / (1.0 + jnp.exp(-g)) * u,)


def _gelu(z):
    return 0.5 * z * (1.0 + lax.erf(z * (2.0 ** -0.5)))


def _gelu2_epilogue(zu, zv):
    return (_gelu(zu), _gelu(zv))


def _kv_epilogue(zk, zv):
    return (zk, zk, zv, zv)


def _scaled_epilogue(z, *, factor):
    return (z * factor,)


def _mm_res_body(*refs, n_h, n_k, bk, n_tiles, coef):
    ap_ref, as_ref, w_ref, xp_ref, xs_ref, gpost_ref, gnext_ref = refs[:7]
    out_refs = refs[7:7 + 2 * (1 + n_h)]
    wbf_ref, y0_ref = refs[7 + 2 * (1 + n_h):]
    i = pl.program_id(0)

    @pl.when(i < n_k)
    def _():
        r0 = pl.multiple_of(i * bk, bk)
        w_chunk = w_ref[...].astype(BF16)
        wbf_ref[pl.ds(r0, bk), :] = w_chunk
        part = _dot(ap_ref[:, pl.ds(r0, bk)].astype(BF16), w_chunk)

        @pl.when(i == 0)
        def _():
            y0_ref[...] = part

        @pl.when(i > 0)
        def _():
            y0_ref[...] += part

    def run(a_ref, x_ref, o_refs, y=None):
        if y is None:
            y = _dot(a_ref[...].astype(BF16), wbf_ref[...])
        x_new = x_ref[...] + coef * _rms(y, gpost_ref[...])
        o_refs[0][...] = x_new
        if n_h:
            xn = x_new * lax.rsqrt(jnp.mean(x_new * x_new, axis=-1, keepdims=True) + RMS_EPS)
            for j in range(n_h):
                o_refs[1 + j][...] = (xn * gnext_ref[j:j + 1, :]).astype(BF16)

    @pl.when(i == n_k)
    def _():
        run(ap_ref, xp_ref, out_refs[0::2], y0_ref[...])

    @pl.when(jnp.logical_and(i > n_k, i < n_k + n_tiles))
    def _():
        run(ap_ref, xp_ref, out_refs[0::2])

    @pl.when(i == n_k + n_tiles)
    def _():
        run(as_ref, xs_ref, out_refs[1::2])


def _mm_res(a_p, a_s, w, lead, x_p, x_s, g_post, g_next, coef, *, bm, bk, name):
    mp, k = a_p.shape
    ms = a_s.shape[0]
    d = w.shape[-1]
    n_h = 0 if g_next is None else g_next.shape[0]
    g_next_arr = jnp.zeros((1, d), F32) if g_next is None else g_next
    n_tiles = mp // bm
    n_k = k // bk
    assert n_k * bk == k
    row = lambda i: (jnp.clip(i - n_k, 0, n_tiles - 1), 0)
    fixed = lambda i: (0, 0)
    in_specs = [pl.BlockSpec((bm, k), row), pl.BlockSpec((ms, k), fixed),
                _weight_spec(w, lead, (bk, d), lambda i: (jnp.minimum(i, n_k - 1), 0)),
                pl.BlockSpec((bm, d), row), pl.BlockSpec((ms, d), fixed),
                pl.BlockSpec((1, d), fixed), pl.BlockSpec(g_next_arr.shape, fixed)]
    out_specs = [pl.BlockSpec((bm, d), row), pl.BlockSpec((ms, d), fixed)]
    out_shape = [jax.ShapeDtypeStruct((mp, d), F32), jax.ShapeDtypeStruct((ms, d), F32)]
    for _ in range(n_h):
        out_specs += [pl.BlockSpec((bm, d), row), pl.BlockSpec((ms, d), fixed)]
        out_shape += [jax.ShapeDtypeStruct((mp, d), BF16), jax.ShapeDtypeStruct((ms, d), BF16)]
    est = (k * d * 2 + 2 * bk * d * 4 + 2 * bm * k * a_p.dtype.itemsize + 4 * bm * d * 4
           + n_h * 2 * bm * d * 2 + 3 * bm * d * 4)
    return pl.pallas_call(
        functools.partial(_mm_res_body, n_h=n_h, n_k=n_k, bk=bk, n_tiles=n_tiles, coef=coef),
        grid=(n_k + n_tiles + 1,),
        in_specs=in_specs, out_specs=out_specs, out_shape=out_shape,
        scratch_shapes=[pltpu.VMEM((k, d), BF16), pltpu.VMEM((bm, d), F32)],
        compiler_params=_params(est, 1),
        name=name,
    )(a_p, a_s, w, x_p, x_s, g_post.reshape(1, d), g_next_arr)


def _sgu_body(up_ref, vp_ref, us_ref, vs_ref, lng_ref, lnb_ref, wp_ref, bp_ref, ws_ref, bs_ref,
              pp_ref, ps_ref, vtp_ref, vts_ref, *, n_chunks):
    c = pl.program_id(0)

    def run(u_ref, v_ref, w_ref, b_ref, p_ref, vt_ref):
        rows = v_ref.shape[0]
        gw = v_ref.shape[1] // N_GROUPS
        v = v_ref[...]
        xc = v - jnp.mean(v, axis=-1, keepdims=True)
        var = jnp.mean(xc * xc, axis=-1, keepdims=True)
        vn = xc * lax.rsqrt(var + LN_EPS) * lng_ref[...] + lnb_ref[...]
        vt_ref[...] = vn
        vb = vn.astype(BF16)
        causal = (lax.broadcasted_iota(jnp.int32, (rows, rows), 0)
                  >= lax.broadcasted_iota(jnp.int32, (rows, rows), 1))
        for g in range(N_GROUPS):
            wm = jnp.where(causal, w_ref[g], 0.0).astype(BF16)
            mixed = _dot(wm, vb[:, g * gw:(g + 1) * gw]) + b_ref[:, g:g + 1]
            p_ref[:, g * gw:(g + 1) * gw] = (
                u_ref[:, g * gw:(g + 1) * gw].astype(F32) * mixed).astype(BF16)

    @pl.when(c < n_chunks)
    def _():
        run(up_ref, vp_ref, wp_ref, bp_ref, pp_ref, vtp_ref)

    @pl.when(c == n_chunks)
    def _():
        run(us_ref, vs_ref, ws_ref, bs_ref, ps_ref, vts_ref)


def _sgu_mix(u_p, v_p, u_s, v_s, ln_g, ln_b, w_sp, b_sp, batch, dec_batch):
    mp, du = u_p.shape
    ms = u_s.shape[0]
    n_chunks = mp // CHUNK
    chunks_per_seq = n_chunks // batch
    dec_seq = ms // dec_batch
    last = n_chunks - 1
    eye = jnp.eye(dec_batch, dtype=F32)
    w_s = (eye[None, :, None, :, None] * w_sp[:, None, :dec_seq, None, :dec_seq]
           ).reshape(N_GROUPS, ms, ms)
    b_s = jnp.tile(jnp.transpose(b_sp[:, :dec_seq]), (dec_batch, 1))
    b_p = jnp.transpose(b_sp)
    row = lambda c: (jnp.minimum(c, last), 0)
    fixed2 = lambda c: (0, 0)
    fixed3 = lambda c: (0, 0, 0)
    est = 2 * CHUNK * du * (2 + 4 + 2 + 4) + 8 * CHUNK * du * 4
    return pl.pallas_call(
        functools.partial(_sgu_body, n_chunks=n_chunks),
        grid=(n_chunks + 1,),
        in_specs=[pl.BlockSpec((CHUNK, du), row), pl.BlockSpec((CHUNK, du), row),
                  pl.BlockSpec((ms, du), fixed2), pl.BlockSpec((ms, du), fixed2),
                  pl.BlockSpec((1, du), fixed2), pl.BlockSpec((1, du), fixed2),
                  pl.BlockSpec((N_GROUPS, CHUNK, CHUNK), fixed3), pl.BlockSpec((CHUNK, N_GROUPS), fixed2),
                  pl.BlockSpec((N_GROUPS, ms, ms), fixed3), pl.BlockSpec((ms, N_GROUPS), fixed2)],
        out_specs=[pl.BlockSpec((CHUNK, du), row), pl.BlockSpec((ms, du), fixed2),
                   pl.BlockSpec((CHUNK, du), lambda c: (jnp.minimum(c, last) // chunks_per_seq, 0)),
                   pl.BlockSpec((ms, du), fixed2)],
        out_shape=[jax.ShapeDtypeStruct((mp, du), BF16), jax.ShapeDtypeStruct((ms, du), BF16),
                   jax.ShapeDtypeStruct((batch * CHUNK, du), F32), jax.ShapeDtypeStruct((ms, du), F32)],
        compiler_params=_params(est, 1),
        name="sgu_mix",
    )(u_p, v_p, u_s, v_s, ln_g.reshape(1, du), ln_b.reshape(1, du), w_sp, b_p, w_s, b_s)


def _logf_body(xp_ref, xs_ref, g_ref, w_ref, b_ref, lp_ref, ls_ref, *, n_tiles):
    i = pl.program_id(0)

    def run(x_ref, o_ref):
        z = _dot_exact(_rms(x_ref[...], g_ref[...]), w_ref[...]) + b_ref[...]
        o_ref[...] = jnp.minimum(z, 0.0) - jnp.log1p(jnp.exp(-jnp.abs(z)))

    @pl.when(i < n_tiles)
    def _():
        run(xp_ref, lp_ref)

    @pl.when(i == n_tiles)
    def _():
        run(xs_ref, ls_ref)


def _logf(x_p, x_s, g_kv, w_f, b_f, bm=512):
    mp, d = x_p.shape
    ms = x_s.shape[0]
    nh = w_f.shape[1]
    w_pad = jnp.pad(w_f, ((0, 0), (0, LANES - nh)))
    b_pad = jnp.pad(b_f, (0, LANES - nh)).reshape(1, LANES)
    n_tiles = mp // bm
    last = n_tiles - 1
    row = lambda i: (jnp.minimum(i, last), 0)
    fixed = lambda i: (0, 0)
    return pl.pallas_call(
        functools.partial(_logf_body, n_tiles=n_tiles),
        grid=(n_tiles + 1,),
        in_specs=[pl.BlockSpec((bm, d), row), pl.BlockSpec((ms, d), fixed),
                  pl.BlockSpec((1, d), fixed), pl.BlockSpec((d, LANES), fixed),
                  pl.BlockSpec((1, LANES), fixed)],
        out_specs=[pl.BlockSpec((bm, LANES), row), pl.BlockSpec((ms, LANES), fixed)],
        out_shape=[jax.ShapeDtypeStruct((mp, LANES), F32), jax.ShapeDtypeStruct((ms, LANES), F32)],
        compiler_params=_params(8 * bm * d * 4, 1),
        name="logf",
    )(x_p, x_s, g_kv.reshape(1, d), w_pad, b_pad)


def _cumsum_body(l_ref, c_ref, *, n_chunks):
    tril = (lax.broadcasted_iota(jnp.int32, (CHUNK, CHUNK), 0)
            >= lax.broadcasted_iota(jnp.int32, (CHUNK, CHUNK), 1)).astype(F32)

    def step(j, carry):
        r0 = pl.multiple_of(j * CHUNK, CHUNK)
        local = _dot_exact(tril, l_ref[pl.ds(r0, CHUNK), :]) + carry
        c_ref[pl.ds(r0, CHUNK), :] = local
        return local[CHUNK - 1:CHUNK, :]

    lax.fori_loop(0, n_chunks, step, jnp.zeros((1, LANES), F32))


def _cumsum_rows(lf_p, batch):
    mp = lf_p.shape[0]
    seq = mp // batch
    return pl.pallas_call(
        functools.partial(_cumsum_body, n_chunks=seq // CHUNK),
        grid=(batch,),
        in_specs=[pl.BlockSpec((seq, LANES), lambda b: (b, 0))],
        out_specs=pl.BlockSpec((seq, LANES), lambda b: (b, 0)),
        out_shape=jax.ShapeDtypeStruct((mp, LANES), F32),
        compiler_params=_params(4 * seq * LANES * 4, 1),
        name="cumsum_logf",
    )(lf_p)


def _head_column(c, h):
    lane = lax.broadcasted_iota(jnp.int32, (1, LANES), 1)
    return jnp.sum(jnp.where(lane == h, c, 0.0), axis=-1, keepdims=True) * LOG2E


def _split3(c):
    c1 = c.astype(BF16).astype(F32)
    r1 = c - c1
    c2 = r1.astype(BF16).astype(F32)
    c3 = (r1 - c2).astype(BF16).astype(F32)
    return c1, c2, c3


def _bias_columns(c_col, query_side):
    rows = c_col.shape[0]
    lane = lax.broadcasted_iota(jnp.int32, (rows, LANES), 1)
    c1, c2, c3 = _split3(c_col)
    if query_side:
        aug = jnp.where(lane == 0, c1, jnp.where(lane == 1, c2, jnp.where(lane == 2, c3, jnp.where(lane < 6, 1.0, 0.0))))
    else:
        aug = jnp.where(lane < 3, 1.0, jnp.where(lane == 3, -c1, jnp.where(lane == 4, -c2, jnp.where(lane == 5, -c3, 0.0))))
    return aug.astype(BF16)


def _fox_prompt_body(q_ref, k_ref, v_ref, cq_ref, ck_ref, o_ref, kaug_ref, vt_ref, *, tq, tk, hs, nq):
    hg = pl.program_id(1)
    qi = pl.program_id(2)
    head_cols = [slice(a * HEAD_DIM, (a + 1) * HEAD_DIM) for a in range(hs)]
    aug_cols = [slice(a * 2 * HEAD_DIM, (a + 1) * 2 * HEAD_DIM) for a in range(hs)]

    @pl.when(qi == 0)
    def _():
        ck = ck_ref[...]
        for a in range(hs):
            kaug_ref[:, aug_cols[a]] = jnp.concatenate(
                [k_ref[:, head_cols[a]], _bias_columns(_head_column(ck, hg * hs + a), False)], axis=1)
            vt_ref[a] = v_ref[:, head_cols[a]].astype(F32).T.astype(BF16)

    cq = cq_ref[...]
    q_t = [jnp.concatenate([q_ref[:, head_cols[a]].astype(F32),
                            _bias_columns(_head_column(cq, hg * hs + a), True).astype(F32)],
                           axis=1).T.astype(BF16) for a in range(hs)]

    def tile(j, carry, masked):
        r0 = j * tk
        out = []
        for a in range(hs):
            m, l, acc = carry[a]
            s = _dot(kaug_ref[pl.ds(r0, tk), aug_cols[a]], q_t[a])
            if masked:
                s = jnp.where(lax.broadcasted_iota(jnp.int32, (tk, tq), 0)
                              <= lax.broadcasted_iota(jnp.int32, (tk, tq), 1), s, NEG_INF)
            m_new = jnp.maximum(m, jnp.max(s, axis=0, keepdims=True))
            alpha = jnp.exp2(m - m_new)
            p = jnp.exp2(s - m_new)
            l = alpha * l + jnp.sum(p, axis=0, keepdims=True)
            acc = alpha * acc + _dot(vt_ref[a, :, pl.ds(r0, tk)], p.astype(BF16))
            out.append((m_new, l, acc))
        return tuple(out)

    for n_full in range(nq):
        @pl.when(qi == n_full)
        def _(n_full=n_full):
            carry = tuple((jnp.full((1, tq), NEG_INF, F32), jnp.zeros((1, tq), F32),
                           jnp.zeros((HEAD_DIM, tq), F32)) for _ in range(hs))
            for j in range(n_full):
                carry = tile(j, carry, False)
            carry = tile(n_full, carry, True)
            for a in range(hs):
                _, l, acc = carry[a]
                o_ref[:, head_cols[a]] = (acc / l).T.astype(o_ref.dtype)


def _fox_prompt(q_p, k_bf, v_bf, c_p, batch, tq=1024, tk=1024, hs=2):
    assert tq == tk
    mp, width = q_p.shape
    n_heads = width // HEAD_DIM
    seq = mp // batch
    nq = seq // tq
    hw = hs * HEAD_DIM
    est = 2 * (2 * seq * hw * 2) + 3 * seq * LANES * 4 + 3 * seq * hw * 2 + hs * 6 * tq * tk * 4
    return pl.pallas_call(
        functools.partial(_fox_prompt_body, tq=tq, tk=tk, hs=hs, nq=nq),
        grid=(batch, n_heads // hs, nq),
        in_specs=[pl.BlockSpec((tq, hw), lambda b, h, i: (b * nq + i, h)),
                  pl.BlockSpec((seq, hw), lambda b, h, i: (b, h)),
                  pl.BlockSpec((seq, hw), lambda b, h, i: (b, h)),
                  pl.BlockSpec((tq, LANES), lambda b, h, i: (b * nq + i, 0)),
                  pl.BlockSpec((seq, LANES), lambda b, h, i: (b, 0))],
        out_specs=pl.BlockSpec((tq, hw), lambda b, h, i: (b * nq + i, h)),
        out_shape=jax.ShapeDtypeStruct((mp, width), BF16),
        scratch_shapes=[pltpu.VMEM((seq, 2 * hw), BF16), pltpu.VMEM((hs, HEAD_DIM, seq), BF16)],
        compiler_params=_params(est, 3),
        name="fox_prompt",
    )(q_p, k_bf, v_bf, c_p, c_p)


def _ck_past_body(pt_ref, *refs, n_group, n_heads, dec_seq):
    lf_refs = refs[:n_group]
    out_ref, carry_ref = refs[n_group:]
    p = pl.program_id(1)

    @pl.when(p == 0)
    def _():
        carry_ref[...] = jnp.zeros_like(carry_ref)

    expand = (lax.broadcasted_iota(jnp.int32, (n_heads, LANES), 1) // dec_seq
              == lax.broadcasted_iota(jnp.int32, (n_heads, LANES), 0)).astype(BF16)
    later = (lax.broadcasted_iota(jnp.int32, (PAGE, PAGE), 1)
             > lax.broadcasted_iota(jnp.int32, (PAGE, PAGE), 0)).astype(BF16)
    lf = jnp.concatenate([lf_refs[g][0] for g in range(n_group)], axis=0)
    lf_e, within = None, None
    for term in _split3(lf):
        e = _dot(term.astype(BF16), expand)
        e_by_lanes = jnp.concatenate([e[g * PAGE:(g + 1) * PAGE] for g in range(n_group)], axis=1)
        w = _dot(later, e_by_lanes.astype(BF16))
        lf_e = e if lf_e is None else lf_e + e
        within = w if within is None else within + w
    carry = carry_ref[...]
    for g in range(n_group):
        within_g = within[:, g * LANES:(g + 1) * LANES]
        out_ref[0, n_group - 1 - g] = -(within_g + carry)
        carry = carry + within_g[0:1, :] + lf_e[g * PAGE:g * PAGE + 1, :]
    carry_ref[...] = carry


def _ck_past(cache_logf, page_table, dec_seq, n_group=32):
    n_seq, n_pages = page_table.shape
    n_heads = cache_logf.shape[2]
    assert n_heads * dec_seq == LANES and n_pages % n_group == 0
    lf_map = lambda b, p, pt, g: (pt[b, n_pages - 1 - (p * n_group + g)], 0, 0)
    n_steps = n_pages // n_group
    grid_spec = pltpu.PrefetchScalarGridSpec(
        num_scalar_prefetch=1,
        grid=(n_seq, n_steps),
        in_specs=[pl.BlockSpec((1, PAGE, n_heads), functools.partial(lf_map, g=g))
                  for g in range(n_group)],
        out_specs=pl.BlockSpec((1, n_group, PAGE, LANES), lambda b, p, pt: (b, n_steps - 1 - p, 0, 0)),
        scratch_shapes=[pltpu.VMEM((1, LANES), F32)])
    return pl.pallas_call(
        functools.partial(_ck_past_body, n_group=n_group, n_heads=n_heads, dec_seq=dec_seq),
        grid_spec=grid_spec,
        out_shape=jax.ShapeDtypeStruct((n_seq, n_pages, PAGE, LANES), F32),
        compiler_params=_params(64 * PAGE * LANES * 4, 2),
        name="ck_past",
    )(page_table, *([cache_logf] * n_group))


def _fox_sample_body(pt_ref, *refs, n_group, n_heads, dec_seq):
    k_refs = refs[:n_group]
    v_refs = refs[n_group:2 * n_group]
    ck_ref, qt_ref, kn_ref, vn_ref, lfn_ref, o_ref, m_ref, l_ref, acc_ref, cq_ref = refs[2 * n_group:]
    p = pl.program_id(1)
    n_steps = pl.num_programs(1)
    qt = qt_ref[0]
    eye = (lax.broadcasted_iota(jnp.int32, (LANES, LANES), 0)
           == lax.broadcasted_iota(jnp.int32, (LANES, LANES), 1))
    head_match = (lax.broadcasted_iota(jnp.int32, (n_heads, LANES), 1) // dec_seq
                  == lax.broadcasted_iota(jnp.int32, (n_heads, LANES), 0))

    def to_column(row):
        return jnp.sum(jnp.where(eye, row, 0.0), axis=-1, keepdims=True)

    def scores(rows_ref, bias, keep):
        s = _dot(rows_ref[0].astype(BF16), qt)
        s = s.reshape(bias.shape[0], n_heads, LANES) + bias[:, None, :]
        return jnp.where(keep, s, NEG_INF)

    def col_max(s3):
        return jnp.max(jnp.max(s3, axis=0), axis=0, keepdims=True)

    def col_sum(p3):
        return jnp.sum(jnp.sum(p3, axis=0), axis=0, keepdims=True)

    def weighted_values(p3, rows_ref):
        pr = p3.reshape(p3.shape[0] * n_heads, LANES).astype(BF16)
        return lax.dot_general(pr, rows_ref[0].astype(BF16), (((0,), (0,)), ((), ())),
                               preferred_element_type=F32)

    @pl.when(p == 0)
    def _():
        expand = (lax.broadcasted_iota(jnp.int32, (LANES, LANES), 1) // dec_seq
                  == lax.broadcasted_iota(jnp.int32, (LANES, LANES), 0)).astype(F32)
        lf_e = _dot_exact(lfn_ref[...], expand)
        rows = [lf_e[0:1, :]]
        for j in range(1, dec_seq):
            rows.append(rows[-1] + lf_e[j:j + 1, :])
        c_new = jnp.concatenate(rows, axis=0)
        key_j = lax.broadcasted_iota(jnp.int32, (dec_seq, LANES), 0)
        col_t = lax.broadcasted_iota(jnp.int32, (dec_seq, LANES), 1) % dec_seq
        cq = jnp.sum(jnp.where(key_j == col_t, c_new, 0.0), axis=0, keepdims=True)
        cq_ref[...] = cq
        causal = (key_j <= col_t)[:, None, :]
        s3 = scores(kn_ref, (cq - c_new) * LOG2E, jnp.logical_and(causal, head_match[None]))
        m = col_max(s3)
        p3 = jnp.exp2(s3 - m[None])
        m_ref[...] = m
        l_ref[...] = col_sum(p3)
        acc_ref[...] = weighted_values(p3, vn_ref)

    cq = cq_ref[...]
    s_pages = [scores(k_refs[g], (cq - ck_ref[0, n_group - 1 - g]) * LOG2E, head_match[None])
               for g in range(n_group)]
    m_old = m_ref[...]
    m_new = m_old
    for s3 in s_pages:
        m_new = jnp.maximum(m_new, col_max(s3))
    alpha = jnp.exp2(m_old - m_new)
    l_new = alpha * l_ref[...]
    pv = None
    for g in range(n_group):
        p3 = jnp.exp2(s_pages[g] - m_new[None])
        l_new = l_new + col_sum(p3)
        t = weighted_values(p3, v_refs[g])
        pv = t if pv is None else pv + t
    m_ref[...] = m_new
    l_ref[...] = l_new
    acc_ref[...] = to_column(alpha) * acc_ref[...] + pv

    @pl.when(p == n_steps - 1)
    def _():
        out = acc_ref[...] / to_column(l_ref[...])
        for h in range(n_heads):
            o_ref[:, h * HEAD_DIM:(h + 1) * HEAD_DIM] = out[h * dec_seq:(h + 1) * dec_seq, :]


def _fox_sample(q_s, k_s, v_s, lf_s, cache_k, cache_v, ck_pages, page_table, dec_seq, n_group=8):
    n_seq, n_pages = page_table.shape
    n_heads = cache_k.shape[2]
    width = n_heads * HEAD_DIM
    page_rows = PAGE * n_heads
    new_rows = dec_seq * n_heads
    n_steps = n_pages // n_group
    ck_rows = cache_k.reshape(cache_k.shape[0], page_rows, HEAD_DIM)
    cv_rows = cache_v.reshape(cache_v.shape[0], page_rows, HEAD_DIM)
    kn_rows = k_s.reshape(n_seq, new_rows, HEAD_DIM)
    vn_rows = v_s.reshape(n_seq, new_rows, HEAD_DIM)
    qt = jnp.transpose(q_s.reshape(n_seq, dec_seq, n_heads, HEAD_DIM), (0, 3, 2, 1))
    qt = qt.reshape(n_seq, HEAD_DIM, n_heads * dec_seq).astype(BF16)
    page_map = lambda b, p, pt, g: (pt[b, n_pages - 1 - (p * n_group + g)], 0, 0)
    seq3 = lambda b, p, pt: (b, 0, 0)
    page_block = (1, page_rows, HEAD_DIM)
    in_specs = ([pl.BlockSpec(page_block, functools.partial(page_map, g=g)) for g in range(n_group)]
                + [pl.BlockSpec(page_block, functools.partial(page_map, g=g)) for g in range(n_group)]
                + [pl.BlockSpec((1, n_group, PAGE, LANES), lambda b, p, pt: (b, n_steps - 1 - p, 0, 0)),
                   pl.BlockSpec((1, HEAD_DIM, LANES), seq3),
                   pl.BlockSpec((1, new_rows, HEAD_DIM), seq3), pl.BlockSpec((1, new_rows, HEAD_DIM), seq3),
                   pl.BlockSpec((dec_seq, LANES), lambda b, p, pt: (b, 0))])
    grid_spec = pltpu.PrefetchScalarGridSpec(
        num_scalar_prefetch=1,
        grid=(n_seq, n_steps),
        in_specs=in_specs,
        out_specs=pl.BlockSpec((dec_seq, width), lambda b, p, pt: (b, 0)),
        scratch_shapes=[pltpu.VMEM((1, LANES), F32), pltpu.VMEM((1, LANES), F32),
                        pltpu.VMEM((LANES, HEAD_DIM), F32), pltpu.VMEM((1, LANES), F32)])
    est = 2 * 2 * n_group * page_rows * HEAD_DIM * 4 + n_group * page_rows * LANES * 12
    return pl.pallas_call(
        functools.partial(_fox_sample_body, n_group=n_group, n_heads=n_heads, dec_seq=dec_seq),
        grid_spec=grid_spec,
        out_shape=jax.ShapeDtypeStruct((n_seq * dec_seq, width), F32),
        compiler_params=_params(est, 2),
        name="fox_sample",
    )(page_table, *([ck_rows] * n_group), *([cv_rows] * n_group), ck_pages, qt, kn_rows, vn_rows, lf_s)


def kernel(x_prompt, x_sample, cache_k, cache_v, cache_logf, page_table, g_pre, g_post, w_ffn_gate, w_ffn_up, w_ffn_down, w_a_in, a_ln_g, a_ln_b, w_a_spatial, b_a_spatial, w_a_out, g_kv, w_kvf, b_f, w_q, w_o):
    batch, seq, d_model = x_prompt.shape
    dec_batch, dec_seq, _ = x_sample.shape
    depth = g_pre.shape[0]
    n_a = w_a_in.shape[0]
    d_ff = w_ffn_gate.shape[-1]
    d_u = a_ln_g.shape[-1]
    n_heads = b_f.shape[0]
    width = n_heads * HEAD_DIM
    q_factor = LOG2E * HEAD_DIM ** -0.5

    x_p = x_prompt.reshape(batch * seq, d_model)
    x_s = x_sample.reshape(dec_batch * dec_seq, d_model)
    h_p, h_s = _prenorm(x_p, x_s, g_pre[0, 0])

    def ffn(x_p, x_s, h_p, h_s, l, j, g_next):
        a_p, a_s = _mm_ws(h_p, h_s, [(w_ffn_gate, (l, j), 0), (w_ffn_up, (l, j), 0)], d_ff,
                          _swiglu_epilogue, [(BF16, BF16)], name="ffn_gate_up")
        return _mm_res(a_p, a_s, w_ffn_down, (l, j), x_p, x_s, g_post[l, 2 * j], g_next,
                       0.5, bm=256, bk=512, name="ffn_down")

    sgu_p, sgu_s = [], []
    kv_out = None
    for l in range(depth):
        if l == n_a:
            k_p, k_s, kb_p, _, v_p, v_s, vb_p, _ = _mm_ws(
                hk_p, hk_s, [(w_kvf, (), 0), (w_kvf, (), width)], width, _kv_epilogue,
                [(F32, F32), (BF16, BF16), (F32, F32), (BF16, BF16)], name="proj_kv")
            lf_p, lf_s = _logf(x_p, x_s, g_kv, w_kvf[:, 2 * width:], b_f)
            c_p = _cumsum_rows(lf_p, batch)
            ck_pages = _ck_past(cache_logf, page_table, dec_seq)
            kv_out = (k_p, k_s, v_p, v_s, lf_p, lf_s)

        x_p, x_s, h_p, h_s = ffn(x_p, x_s, h_p, h_s, l, 0, g_pre[l, 1:2])

        if l < n_a:
            u_p, u_s, v_p_raw, v_s_raw = _mm_ws(
                h_p, h_s, [(w_a_in, (l,), 0), (w_a_in, (l,), d_u)], d_u, _gelu2_epilogue,
                [(BF16, BF16), (F32, F32)], name="sgu_in")
            m_p, m_s, vt_p, vt_s = _sgu_mix(u_p, v_p_raw, u_s, v_s_raw, a_ln_g[l], a_ln_b[l],
                                            w_a_spatial[l], b_a_spatial[l], batch, dec_batch)
            sgu_p.append(vt_p.reshape(batch, CHUNK, d_u))
            sgu_s.append(vt_s.reshape(dec_batch, dec_seq, d_u))
            w_mix, lead, bk = w_a_out, (l,), 512
        else:
            b = l - n_a
            q_p, q_s = _mm_ws(h_p, h_s, [(w_q, (b,), 0)], width,
                              functools.partial(_scaled_epilogue, factor=q_factor), [(BF16, F32)],
                              bn=1024, name="proj_q")
            m_p = _fox_prompt(q_p, kb_p, vb_p, c_p, batch)
            m_s = _fox_sample(q_s, k_s, v_s, lf_s, cache_k, cache_v, ck_pages, page_table, dec_seq)
            w_mix, lead, bk = w_o, (b,), 512
        x_p, x_s, h_p, h_s = _mm_res(m_p, m_s, w_mix, lead, x_p, x_s, g_post[l, 1], g_pre[l, 2:3], 1.0,
                                     bm=256 if l < n_a else 512, bk=bk, name="mixer_out")

        if l + 1 < depth:
            g_next = g_pre[l + 1, 0:1]
            if l + 1 == n_a:
                g_next = jnp.concatenate([g_next, g_kv.reshape(1, d_model)], axis=0)
            outs = ffn(x_p, x_s, h_p, h_s, l, 1, g_next)
            x_p, x_s, h_p, h_s = outs[:4]
            if l + 1 == n_a:
                hk_p, hk_s = outs[4:6]
        else:
            x_p, x_s = ffn(x_p, x_s, h_p, h_s, l, 1, None)

    k_p, k_s, v_p, v_s, lf_p, lf_s = kv_out
    return (x_p.reshape(batch, seq, d_model),
            x_s.reshape(dec_batch, dec_seq, d_model),
            k_p.reshape(batch, seq, n_heads, HEAD_DIM),
            v_p.reshape(batch, seq, n_heads, HEAD_DIM),
            lf_p[:, :n_heads].reshape(batch, seq, n_heads),
            k_s.reshape(dec_batch, dec_seq, n_heads, HEAD_DIM),
            v_s.reshape(dec_batch, dec_seq, n_heads, HEAD_DIM),
            lf_s[:, :n_heads].reshape(dec_batch, dec_seq, n_heads),
            jnp.stack(sgu_p),
            jnp.stack(sgu_s))
```

```python
import functools

import jax
import jax.numpy as jnp
from jax import lax
from jax.experimental import pallas as pl
from jax.experimental.pallas import tpu as pltpu

F32 = jnp.float32
BF16 = jnp.bfloat16

RMS_EPS = 1e-6
LN_EPS = 1e-5
NEG_INF = -1e30
LOG2E = 1.4426950408889634
CHUNK = 128
N_GROUPS = 16
HEAD_DIM = 128
PAGE = 128

V7X_VMEM_LIMIT_CAP = 56 * 1024 * 1024
LANES = 128


def _vmem_limit(estimate_bytes):
    return int(min(max(estimate_bytes * 5 // 4, 16 * 1024 * 1024), V7X_VMEM_LIMIT_CAP))


def _params(estimate_bytes, n_grid_dims):
    return pltpu.CompilerParams(
        dimension_semantics=("arbitrary",) * n_grid_dims,
        vmem_limit_bytes=_vmem_limit(estimate_bytes))


def _rms(x, g):
    return x * lax.rsqrt(jnp.mean(x * x, axis=-1, keepdims=True) + RMS_EPS) * g


def _dot(a, b):
    return jnp.dot(a, b, preferred_element_type=F32)


def _dot_exact(a, b):
    return jnp.dot(a, b, preferred_element_type=F32, precision=lax.Precision.HIGHEST)


def _weight_spec(w, lead, block, index):
    n_lead = len(lead)
    assert w.ndim == n_lead + 2
    return pl.BlockSpec((None,) * n_lead + block, lambda *g: tuple(lead) + index(*g))


def _prenorm_body(xp_ref, xs_ref, g_ref, hp_ref, hs_ref, *, n_tiles):
    i = pl.program_id(0)

    @pl.when(i < n_tiles)
    def _():
        hp_ref[...] = _rms(xp_ref[...], g_ref[...]).astype(BF16)

    @pl.when(i == n_tiles)
    def _():
        hs_ref[...] = _rms(xs_ref[...], g_ref[...]).astype(BF16)


def _prenorm(x_p, x_s, g, bm=1024):
    mp, d = x_p.shape
    ms = x_s.shape[0]
    n_tiles = mp // bm
    last = n_tiles - 1
    row = lambda i: (jnp.minimum(i, last), 0)
    fixed = lambda i: (0, 0)
    return pl.pallas_call(
        functools.partial(_prenorm_body, n_tiles=n_tiles),
        grid=(n_tiles + 1,),
        in_specs=[pl.BlockSpec((bm, d), row), pl.BlockSpec((ms, d), fixed),
                  pl.BlockSpec((1, d), fixed)],
        out_specs=[pl.BlockSpec((bm, d), row), pl.BlockSpec((ms, d), fixed)],
        out_shape=[jax.ShapeDtypeStruct((mp, d), BF16), jax.ShapeDtypeStruct((ms, d), BF16)],
        compiler_params=_params(6 * bm * d * 4, 1),
        name="prenorm",
    )(x_p, x_s, g.reshape(1, d))


def _mm_ws_body(*refs, n_w, n_out, epilogue):
    hp_ref, hs_ref = refs[0], refs[1]
    w_refs = refs[2:2 + n_w]
    out_refs = refs[2 + n_w:2 + n_w + 2 * n_out]
    wbf_refs = refs[2 + n_w + 2 * n_out:]
    m = pl.program_id(1)

    def run(h_ref, o_refs):
        h = h_ref[...]
        outs = epilogue(*[_dot(h, wbf_ref[...]) for wbf_ref in wbf_refs])
        for o_ref, o in zip(o_refs, outs):
            o_ref[...] = o.astype(o_ref.dtype)

    @pl.when(m == 0)
    def _():
        for w_ref, wbf_ref in zip(w_refs, wbf_refs):
            wbf_ref[...] = w_ref[...].astype(BF16)
        run(hs_ref, out_refs[1::2])

    @pl.when(m > 0)
    def _():
        run(hp_ref, out_refs[0::2])


def _mm_ws(h_p, h_s, weights, n_cols, epilogue, out_dtypes, *, bm=1024, bn=512, name):
    mp, k = h_p.shape
    ms = h_s.shape[0]
    n_tiles = mp // bm
    n_w, n_out = len(weights), len(out_dtypes)
    prow = lambda m: jnp.maximum(m - 1, 0)
    in_specs = [pl.BlockSpec((bm, k), lambda n, m: (prow(m), 0)),
                pl.BlockSpec((ms, k), lambda n, m: (0, 0))]
    for w, lead, off in weights:
        assert off % bn == 0
        in_specs.append(_weight_spec(w, lead, (k, bn), lambda n, m, o=off // bn: (0, n + o)))
    out_specs, out_shape = [], []
    for dt_p, dt_s in out_dtypes:
        out_specs += [pl.BlockSpec((bm, bn), lambda n, m: (prow(m), n)),
                      pl.BlockSpec((ms, bn), lambda n, m: (0, n))]
        out_shape += [jax.ShapeDtypeStruct((mp, n_cols), dt_p),
                      jax.ShapeDtypeStruct((ms, n_cols), dt_s)]
    est = (2 * bm * k * 2 + n_w * k * bn * (2 * 4 + 2)
           + n_out * 2 * bm * bn * 4 + (n_w + n_out) * bm * bn * 4)
    return pl.pallas_call(
        functools.partial(_mm_ws_body, n_w=n_w, n_out=n_out, epilogue=epilogue),
        grid=(n_cols // bn, n_tiles + 1),
        in_specs=in_specs, out_specs=out_specs, out_shape=out_shape,
        scratch_shapes=[pltpu.VMEM((k, bn), BF16) for _ in weights],
        compiler_params=_params(est, 2),
        name=name,
    )(h_p, h_s, *[w for w, _, _ in weights])


def _swiglu_epilogue(g, u):
    return (g / (1.0 + jnp.exp(-g)) * u,)


def _gelu(z):
    return 0.5 * z * (1.0 + lax.erf(z * (2.0 ** -0.5)))


def _gelu2_epilogue(zu, zv):
    return (_gelu(zu), _gelu(zv))


def _kv_epilogue(zk, zv):
    return (zk, zk, zv, zv)


def _scaled_epilogue(z, *, factor):
    return (z * factor,)


def _mm_res_body(*refs, n_h, n_k, bk, n_tiles, coef):
    ap_ref, as_ref, w_ref, xp_ref, xs_ref, gpost_ref, gnext_ref = refs[:7]
    out_refs = refs[7:7 + 2 * (1 + n_h)]
    wbf_ref, y0_ref = refs[7 + 2 * (1 + n_h):]
    i = pl.program_id(0)

    @pl.when(i < n_k)
    def _():
        r0 = pl.multiple_of(i * bk, bk)
        w_chunk = w_ref[...].astype(BF16)
        wbf_ref[pl.ds(r0, bk), :] = w_chunk
        part = _dot(ap_ref[:, pl.ds(r0, bk)].astype(BF16), w_chunk)

        @pl.when(i == 0)
        def _():
            y0_ref[...] = part

        @pl.when(i > 0)
        def _():
            y0_ref[...] += part

    def run(a_ref, x_ref, o_refs, y=None):
        if y is None:
            y = _dot(a_ref[...].astype(BF16), wbf_ref[...])
        x_new = x_ref[...] + coef * _rms(y, gpost_ref[...])
        o_refs[0][...] = x_new
        if n_h:
            xn = x_new * lax.rsqrt(jnp.mean(x_new * x_new, axis=-1, keepdims=True) + RMS_EPS)
            for j in range(n_h):
                o_refs[1 + j][...] = (xn * gnext_ref[j:j + 1, :]).astype(BF16)

    @pl.when(i == n_k)
    def _():
        run(ap_ref, xp_ref, out_refs[0::2], y0_ref[...])

    @pl.when(jnp.logical_and(i > n_k, i < n_k + n_tiles))
    def _():
        run(ap_ref, xp_ref, out_refs[0::2])

    @pl.when(i == n_k + n_tiles)
    def _():
        run(as_ref, xs_ref, out_refs[1::2])


def _mm_res(a_p, a_s, w, lead, x_p, x_s, g_post, g_next, coef, *, bm, bk, name):
    mp, k = a_p.shape
    ms = a_s.shape[0]
    d = w.shape[-1]
    n_h = 0 if g_next is None else g_next.shape[0]
    g_next_arr = jnp.zeros((1, d), F32) if g_next is None else g_next
    n_tiles = mp // bm
    n_k = k // bk
    assert n_k * bk == k
    row = lambda i: (jnp.clip(i - n_k, 0, n_tiles - 1), 0)
    fixed = lambda i: (0, 0)
    in_specs = [pl.BlockSpec((bm, k), row), pl.BlockSpec((ms, k), fixed),
                _weight_spec(w, lead, (bk, d), lambda i: (jnp.minimum(i, n_k - 1), 0)),
                pl.BlockSpec((bm, d), row), pl.BlockSpec((ms, d), fixed),
                pl.BlockSpec((1, d), fixed), pl.BlockSpec(g_next_arr.shape, fixed)]
    out_specs = [pl.BlockSpec((bm, d), row), pl.BlockSpec((ms, d), fixed)]
    out_shape = [jax.ShapeDtypeStruct((mp, d), F32), jax.ShapeDtypeStruct((ms, d), F32)]
    for _ in range(n_h):
        out_specs += [pl.BlockSpec((bm, d), row), pl.BlockSpec((ms, d), fixed)]
        out_shape += [jax.ShapeDtypeStruct((mp, d), BF16), jax.ShapeDtypeStruct((ms, d), BF16)]
    est = (k * d * 2 + 2 * bk * d * 4 + 2 * bm * k * a_p.dtype.itemsize + 4 * bm * d * 4
           + n_h * 2 * bm * d * 2 + 3 * bm * d * 4)
    return pl.pallas_call(
        functools.partial(_mm_res_body, n_h=n_h, n_k=n_k, bk=bk, n_tiles=n_tiles, coef=coef),
        grid=(n_k + n_tiles + 1,),
        in_specs=in_specs, out_specs=out_specs, out_shape=out_shape,
        scratch_shapes=[pltpu.VMEM((k, d), BF16), pltpu.VMEM((bm, d), F32)],
        compiler_params=_params(est, 1),
        name=name,
    )(a_p, a_s, w, x_p, x_s, g_post.reshape(1, d), g_next_arr)


def _sgu_body(up_ref, vp_ref, us_ref, vs_ref, lng_ref, lnb_ref, wp_ref, bp_ref, ws_ref, bs_ref,
              pp_ref, ps_ref, vtp_ref, vts_ref, *, n_steps, cps):
    c = pl.program_id(0)

    def run(u_ref, v_ref, w_ref, b_ref, p_ref, vt_ref, n_sub):
        rows = v_ref.shape[0] // n_sub
        gw = v_ref.shape[1] // N_GROUPS
        causal = (lax.broadcasted_iota(jnp.int32, (rows, rows), 0)
                  >= lax.broadcasted_iota(jnp.int32, (rows, rows), 1))
        for sub in range(n_sub):
            r = slice(sub * rows, (sub + 1) * rows)
            v = v_ref[r, :]
            xc = v - jnp.mean(v, axis=-1, keepdims=True)
            var = jnp.mean(xc * xc, axis=-1, keepdims=True)
            vn = xc * lax.rsqrt(var + LN_EPS) * lng_ref[...] + lnb_ref[...]
            if sub == n_sub - 1:
                vt_ref[...] = vn
            vb = vn.astype(BF16)
            for g in range(N_GROUPS):
                wm = jnp.where(causal, w_ref[g], 0.0).astype(BF16)
                mixed = _dot(wm, vb[:, g * gw:(g + 1) * gw]) + b_ref[:, g:g + 1]
                p_ref[r, g * gw:(g + 1) * gw] = (
                    u_ref[r, g * gw:(g + 1) * gw].astype(F32) * mixed).astype(BF16)

    @pl.when(c < n_steps)
    def _():
        run(up_ref, vp_ref, wp_ref, bp_ref, pp_ref, vtp_ref, cps)

    @pl.when(c == n_steps)
    def _():
        run(us_ref, vs_ref, ws_ref, bs_ref, ps_ref, vts_ref, 1)


def _sgu_mix(u_p, v_p, u_s, v_s, ln_g, ln_b, w_sp, b_sp, batch, dec_batch, cps=2):
    mp, du = u_p.shape
    ms = u_s.shape[0]
    n_steps = mp // (cps * CHUNK)
    steps_per_seq = n_steps // batch
    assert steps_per_seq * batch * cps * CHUNK == mp
    dec_seq = ms // dec_batch
    last = n_steps - 1
    rows = cps * CHUNK
    eye = jnp.eye(dec_batch, dtype=F32)
    w_s = (eye[None, :, None, :, None] * w_sp[:, None, :dec_seq, None, :dec_seq]
           ).reshape(N_GROUPS, ms, ms)
    b_s = jnp.tile(jnp.transpose(b_sp[:, :dec_seq]), (dec_batch, 1))
    b_p = jnp.transpose(b_sp)
    row = lambda c: (jnp.minimum(c, last), 0)
    fixed2 = lambda c: (0, 0)
    fixed3 = lambda c: (0, 0, 0)
    est = 2 * rows * du * (2 + 4 + 2) + 2 * CHUNK * du * 4 + 8 * CHUNK * du * 4
    return pl.pallas_call(
        functools.partial(_sgu_body, n_steps=n_steps, cps=cps),
        grid=(n_steps + 1,),
        in_specs=[pl.BlockSpec((rows, du), row), pl.BlockSpec((rows, du), row),
                  pl.BlockSpec((ms, du), fixed2), pl.BlockSpec((ms, du), fixed2),
                  pl.BlockSpec((1, du), fixed2), pl.BlockSpec((1, du), fixed2),
                  pl.BlockSpec((N_GROUPS, CHUNK, CHUNK), fixed3), pl.BlockSpec((CHUNK, N_GROUPS), fixed2),
                  pl.BlockSpec((N_GROUPS, ms, ms), fixed3), pl.BlockSpec((ms, N_GROUPS), fixed2)],
        out_specs=[pl.BlockSpec((rows, du), row), pl.BlockSpec((ms, du), fixed2),
                   pl.BlockSpec((CHUNK, du), lambda c: (jnp.minimum(c, last) // steps_per_seq, 0)),
                   pl.BlockSpec((ms, du), fixed2)],
        out_shape=[jax.ShapeDtypeStruct((mp, du), BF16), jax.ShapeDtypeStruct((ms, du), BF16),
                   jax.ShapeDtypeStruct((batch * CHUNK, du), F32), jax.ShapeDtypeStruct((ms, du), F32)],
        compiler_params=_params(est, 1),
        name="sgu_mix",
    )(u_p, v_p, u_s, v_s, ln_g.reshape(1, du), ln_b.reshape(1, du), w_sp, b_p, w_s, b_s)


def _logf_body(xp_ref, xs_ref, g_ref, w_ref, b_ref, lp_ref, ls_ref, *, n_tiles):
    i = pl.program_id(0)

    def run(x_ref, o_ref):
        z = _dot_exact(_rms(x_ref[...], g_ref[...]), w_ref[...]) + b_ref[...]
        o_ref[...] = jnp.minimum(z, 0.0) - jnp.log1p(jnp.exp(-jnp.abs(z)))

    @pl.when(i < n_tiles)
    def _():
        run(xp_ref, lp_ref)

    @pl.when(i == n_tiles)
    def _():
        run(xs_ref, ls_ref)


def _logf(x_p, x_s, g_kv, w_f, b_f, bm=512):
    mp, d = x_p.shape
    ms = x_s.shape[0]
    nh = w_f.shape[1]
    w_pad = jnp.pad(w_f, ((0, 0), (0, LANES - nh)))
    b_pad = jnp.pad(b_f, (0, LANES - nh)).reshape(1, LANES)
    n_tiles = mp // bm
    last = n_tiles - 1
    row = lambda i: (jnp.minimum(i, last), 0)
    fixed = lambda i: (0, 0)
    return pl.pallas_call(
        functools.partial(_logf_body, n_tiles=n_tiles),
        grid=(n_tiles + 1,),
        in_specs=[pl.BlockSpec((bm, d), row), pl.BlockSpec((ms, d), fixed),
                  pl.BlockSpec((1, d), fixed), pl.BlockSpec((d, LANES), fixed),
                  pl.BlockSpec((1, LANES), fixed)],
        out_specs=[pl.BlockSpec((bm, LANES), row), pl.BlockSpec((ms, LANES), fixed)],
        out_shape=[jax.ShapeDtypeStruct((mp, LANES), F32), jax.ShapeDtypeStruct((ms, LANES), F32)],
        compiler_params=_params(8 * bm * d * 4, 1),
        name="logf",
    )(x_p, x_s, g_kv.reshape(1, d), w_pad, b_pad)


def _cumsum_body(l_ref, c_ref, *, n_chunks):
    tril = (lax.broadcasted_iota(jnp.int32, (CHUNK, CHUNK), 0)
            >= lax.broadcasted_iota(jnp.int32, (CHUNK, CHUNK), 1)).astype(F32)

    def step(j, carry):
        r0 = pl.multiple_of(j * CHUNK, CHUNK)
        local = _dot_exact(tril, l_ref[pl.ds(r0, CHUNK), :]) + carry
        c_ref[pl.ds(r0, CHUNK), :] = local
        return local[CHUNK - 1:CHUNK, :]

    lax.fori_loop(0, n_chunks, step, jnp.zeros((1, LANES), F32))


def _cumsum_rows(lf_p, batch):
    mp = lf_p.shape[0]
    seq = mp // batch
    return pl.pallas_call(
        functools.partial(_cumsum_body, n_chunks=seq // CHUNK),
        grid=(batch,),
        in_specs=[pl.BlockSpec((seq, LANES), lambda b: (b, 0))],
        out_specs=pl.BlockSpec((seq, LANES), lambda b: (b, 0)),
        out_shape=jax.ShapeDtypeStruct((mp, LANES), F32),
        compiler_params=_params(4 * seq * LANES * 4, 1),
        name="cumsum_logf",
    )(lf_p)


def _head_column(c, h):
    lane = lax.broadcasted_iota(jnp.int32, (1, LANES), 1)
    return jnp.sum(jnp.where(lane == h, c, 0.0), axis=-1, keepdims=True) * LOG2E


def _split3(c):
    c1 = c.astype(BF16).astype(F32)
    r1 = c - c1
    c2 = r1.astype(BF16).astype(F32)
    c3 = (r1 - c2).astype(BF16).astype(F32)
    return c1, c2, c3


def _bias_columns(c_col, query_side):
    rows = c_col.shape[0]
    lane = lax.broadcasted_iota(jnp.int32, (rows, LANES), 1)
    c1, c2, c3 = _split3(c_col)
    if query_side:
        aug = jnp.where(lane == 0, c1, jnp.where(lane == 1, c2, jnp.where(lane == 2, c3, jnp.where(lane < 6, 1.0, 0.0))))
    else:
        aug = jnp.where(lane < 3, 1.0, jnp.where(lane == 3, -c1, jnp.where(lane == 4, -c2, jnp.where(lane == 5, -c3, 0.0))))
    return aug.astype(BF16)


def _fox_prompt_body(q_ref, k_ref, v_ref, cq_ref, ck_ref, o_ref, kaug_ref, vt_ref, *, tq, tk, hs, nq):
    hg = pl.program_id(1)
    qi = pl.program_id(2)
    head_cols = [slice(a * HEAD_DIM, (a + 1) * HEAD_DIM) for a in range(hs)]
    aug_cols = [slice(a * 2 * HEAD_DIM, (a + 1) * 2 * HEAD_DIM) for a in range(hs)]

    @pl.when(qi == 0)
    def _():
        ck = ck_ref[...]
        for a in range(hs):
            kaug_ref[:, aug_cols[a]] = jnp.concatenate(
                [k_ref[:, head_cols[a]], _bias_columns(_head_column(ck, hg * hs + a), False)], axis=1)
            vt_ref[a] = v_ref[:, head_cols[a]].astype(F32).T.astype(BF16)

    cq = cq_ref[...]
    q_t = [jnp.concatenate([q_ref[:, head_cols[a]].astype(F32),
                            _bias_columns(_head_column(cq, hg * hs + a), True).astype(F32)],
                           axis=1).T.astype(BF16) for a in range(hs)]

    def tile(j, carry, masked):
        r0 = j * tk
        out = []
        for a in range(hs):
            m, l, acc = carry[a]
            s = _dot(kaug_ref[pl.ds(r0, tk), aug_cols[a]], q_t[a])
            if masked:
                s = jnp.where(lax.broadcasted_iota(jnp.int32, (tk, tq), 0)
                              <= lax.broadcasted_iota(jnp.int32, (tk, tq), 1), s, NEG_INF)
            m_new = jnp.maximum(m, jnp.max(s, axis=0, keepdims=True))
            alpha = jnp.exp2(m - m_new)
            p = jnp.exp2(s - m_new)
            l = alpha * l + jnp.sum(p, axis=0, keepdims=True)
            acc = alpha * acc + _dot(vt_ref[a, :, pl.ds(r0, tk)], p.astype(BF16))
            out.append((m_new, l, acc))
        return tuple(out)

    for n_full in range(nq):
        @pl.when(qi == n_full)
        def _(n_full=n_full):
            carry = tuple((jnp.full((1, tq), NEG_INF, F32), jnp.zeros((1, tq), F32),
                           jnp.zeros((HEAD_DIM, tq), F32)) for _ in range(hs))
            for j in range(n_full):
                carry = tile(j, carry, False)
            carry = tile(n_full, carry, True)
            for a in range(hs):
                _, l, acc = carry[a]
                o_ref[:, head_cols[a]] = (acc / l).T.astype(o_ref.dtype)


def _fox_prompt(q_p, k_bf, v_bf, c_p, batch, tq=1024, tk=1024, hs=2):
    assert tq == tk
    mp, width = q_p.shape
    n_heads = width // HEAD_DIM
    seq = mp // batch
    nq = seq // tq
    hw = hs * HEAD_DIM
    est = 2 * (2 * seq * hw * 2) + 3 * seq * LANES * 4 + 3 * seq * hw * 2 + hs * 6 * tq * tk * 4
    return pl.pallas_call(
        functools.partial(_fox_prompt_body, tq=tq, tk=tk, hs=hs, nq=nq),
        grid=(batch, n_heads // hs, nq),
        in_specs=[pl.BlockSpec((tq, hw), lambda b, h, i: (b * nq + i, h)),
                  pl.BlockSpec((seq, hw), lambda b, h, i: (b, h)),
                  pl.BlockSpec((seq, hw), lambda b, h, i: (b, h)),
                  pl.BlockSpec((tq, LANES), lambda b, h, i: (b * nq + i, 0)),
                  pl.BlockSpec((seq, LANES), lambda b, h, i: (b, 0))],
        out_specs=pl.BlockSpec((tq, hw), lambda b, h, i: (b * nq + i, h)),
        out_shape=jax.ShapeDtypeStruct((mp, width), BF16),
        scratch_shapes=[pltpu.VMEM((seq, 2 * hw), BF16), pltpu.VMEM((hs, HEAD_DIM, seq), BF16)],
        compiler_params=_params(est, 3),
        name="fox_prompt",
    )(q_p, k_bf, v_bf, c_p, c_p)


def _ck_past_body(pt_ref, *refs, n_group, n_heads, dec_seq):
    lf_refs = refs[:n_group]
    out_ref, carry_ref = refs[n_group:]
    p = pl.program_id(1)

    @pl.when(p == 0)
    def _():
        carry_ref[...] = jnp.zeros_like(carry_ref)

    expand = (lax.broadcasted_iota(jnp.int32, (n_heads, LANES), 1) // dec_seq
              == lax.broadcasted_iota(jnp.int32, (n_heads, LANES), 0)).astype(BF16)
    later = (lax.broadcasted_iota(jnp.int32, (PAGE, PAGE), 1)
             > lax.broadcasted_iota(jnp.int32, (PAGE, PAGE), 0)).astype(BF16)
    lf = jnp.concatenate([lf_refs[g][0] for g in range(n_group)], axis=0)
    lf_e, within = None, None
    for term in _split3(lf):
        e = _dot(term.astype(BF16), expand)
        e_by_lanes = jnp.concatenate([e[g * PAGE:(g + 1) * PAGE] for g in range(n_group)], axis=1)
        w = _dot(later, e_by_lanes.astype(BF16))
        lf_e = e if lf_e is None else lf_e + e
        within = w if within is None else within + w
    carry = carry_ref[...]
    for g in range(n_group):
        within_g = within[:, g * LANES:(g + 1) * LANES]
        out_ref[0, n_group - 1 - g] = -(within_g + carry)
        carry = carry + within_g[0:1, :] + lf_e[g * PAGE:g * PAGE + 1, :]
    carry_ref[...] = carry


def _ck_past(cache_logf, page_table, dec_seq, n_group=64):
    n_seq, n_pages = page_table.shape
    n_heads = cache_logf.shape[2]
    assert n_heads * dec_seq == LANES and n_pages % n_group == 0
    lf_map = lambda b, p, pt, g: (pt[b, n_pages - 1 - (p * n_group + g)], 0, 0)
    n_steps = n_pages // n_group
    grid_spec = pltpu.PrefetchScalarGridSpec(
        num_scalar_prefetch=1,
        grid=(n_seq, n_steps),
        in_specs=[pl.BlockSpec((1, PAGE, n_heads), functools.partial(lf_map, g=g))
                  for g in range(n_group)],
        out_specs=pl.BlockSpec((1, n_group, PAGE, LANES), lambda b, p, pt: (b, n_steps - 1 - p, 0, 0)),
        scratch_shapes=[pltpu.VMEM((1, LANES), F32)])
    return pl.pallas_call(
        functools.partial(_ck_past_body, n_group=n_group, n_heads=n_heads, dec_seq=dec_seq),
        grid_spec=grid_spec,
        out_shape=jax.ShapeDtypeStruct((n_seq, n_pages, PAGE, LANES), F32),
        compiler_params=_params((16 * n_group + 64) * PAGE * LANES * 4, 2),
        name="ck_past",
    )(page_table, *([cache_logf] * n_group))


def _fox_sample_body(pt_ref, *refs, n_group, n_heads, dec_seq):
    k_refs = refs[:n_group]
    v_refs = refs[n_group:2 * n_group]
    ck_ref, qt_ref, kn_ref, vn_ref, lfn_ref, o_ref, m_ref, l_ref, acc_ref, cq_ref = refs[2 * n_group:]
    p = pl.program_id(1)
    n_steps = pl.num_programs(1)
    qt = qt_ref[0]
    eye = (lax.broadcasted_iota(jnp.int32, (LANES, LANES), 0)
           == lax.broadcasted_iota(jnp.int32, (LANES, LANES), 1))
    head_match = (lax.broadcasted_iota(jnp.int32, (n_heads, LANES), 1) // dec_seq
                  == lax.broadcasted_iota(jnp.int32, (n_heads, LANES), 0))

    def to_column(row):
        return jnp.sum(jnp.where(eye, row, 0.0), axis=-1, keepdims=True)

    def scores(rows_ref, bias, keep):
        s = _dot(rows_ref[0].astype(BF16), qt)
        s = s.reshape(bias.shape[0], n_heads, LANES) + bias[:, None, :]
        return jnp.where(keep, s, NEG_INF)

    def col_max(s3):
        return jnp.max(jnp.max(s3, axis=0), axis=0, keepdims=True)

    def col_sum(p3):
        return jnp.sum(jnp.sum(p3, axis=0), axis=0, keepdims=True)

    def weighted_values(p3, rows_ref):
        pr = p3.reshape(p3.shape[0] * n_heads, LANES).astype(BF16)
        return lax.dot_general(pr, rows_ref[0].astype(BF16), (((0,), (0,)), ((), ())),
                               preferred_element_type=F32)

    @pl.when(p == 0)
    def _():
        expand = (lax.broadcasted_iota(jnp.int32, (LANES, LANES), 1) // dec_seq
                  == lax.broadcasted_iota(jnp.int32, (LANES, LANES), 0)).astype(F32)
        lf_e = _dot_exact(lfn_ref[...], expand)
        rows = [lf_e[0:1, :]]
        for j in range(1, dec_seq):
            rows.append(rows[-1] + lf_e[j:j + 1, :])
        c_new = jnp.concatenate(rows, axis=0)
        key_j = lax.broadcasted_iota(jnp.int32, (dec_seq, LANES), 0)
        col_t = lax.broadcasted_iota(jnp.int32, (dec_seq, LANES), 1) % dec_seq
        cq = jnp.sum(jnp.where(key_j == col_t, c_new, 0.0), axis=0, keepdims=True)
        cq_ref[...] = cq
        causal = (key_j <= col_t)[:, None, :]
        s3 = scores(kn_ref, (cq - c_new) * LOG2E, jnp.logical_and(causal, head_match[None]))
        m = col_max(s3)
        p3 = jnp.exp2(s3 - m[None])
        m_ref[...] = m
        l_ref[...] = col_sum(p3)
        acc_ref[...] = weighted_values(p3, vn_ref)

    cq = cq_ref[...]
    s_pages = [scores(k_refs[g], (cq - ck_ref[0, n_group - 1 - g]) * LOG2E, head_match[None])
               for g in range(n_group)]
    m_old = m_ref[...]
    m_new = m_old
    for s3 in s_pages:
        m_new = jnp.maximum(m_new, col_max(s3))
    alpha = jnp.exp2(m_old - m_new)
    l_new = alpha * l_ref[...]
    pv = None
    for g in range(n_group):
        p3 = jnp.exp2(s_pages[g] - m_new[None])
        l_new = l_new + col_sum(p3)
        t = weighted_values(p3, v_refs[g])
        pv = t if pv is None else pv + t
    m_ref[...] = m_new
    l_ref[...] = l_new
    acc_ref[...] = to_column(alpha) * acc_ref[...] + pv

    @pl.when(p == n_steps - 1)
    def _():
        out = acc_ref[...] / to_column(l_ref[...])
        for h in range(n_heads):
            o_ref[:, h * HEAD_DIM:(h + 1) * HEAD_DIM] = out[h * dec_seq:(h + 1) * dec_seq, :]


def _fox_sample(q_s, k_s, v_s, lf_s, cache_k, cache_v, ck_pages, page_table, dec_seq, n_group=8):
    n_seq, n_pages = page_table.shape
    n_heads = cache_k.shape[2]
    width = n_heads * HEAD_DIM
    page_rows = PAGE * n_heads
    new_rows = dec_seq * n_heads
    n_steps = n_pages // n_group
    ck_rows = cache_k.reshape(cache_k.shape[0], page_rows, HEAD_DIM)
    cv_rows = cache_v.reshape(cache_v.shape[0], page_rows, HEAD_DIM)
    kn_rows = k_s.reshape(n_seq, new_rows, HEAD_DIM)
    vn_rows = v_s.reshape(n_seq, new_rows, HEAD_DIM)
    qt = jnp.transpose(q_s.reshape(n_seq, dec_seq, n_heads, HEAD_DIM), (0, 3, 2, 1))
    qt = qt.reshape(n_seq, HEAD_DIM, n_heads * dec_seq).astype(BF16)
    page_map = lambda b, p, pt, g: (pt[b, n_pages - 1 - (p * n_group + g)], 0, 0)
    seq3 = lambda b, p, pt: (b, 0, 0)
    page_block = (1, page_rows, HEAD_DIM)
    in_specs = ([pl.BlockSpec(page_block, functools.partial(page_map, g=g)) for g in range(n_group)]
                + [pl.BlockSpec(page_block, functools.partial(page_map, g=g)) for g in range(n_group)]
                + [pl.BlockSpec((1, n_group, PAGE, LANES), lambda b, p, pt: (b, n_steps - 1 - p, 0, 0)),
                   pl.BlockSpec((1, HEAD_DIM, LANES), seq3),
                   pl.BlockSpec((1, new_rows, HEAD_DIM), seq3), pl.BlockSpec((1, new_rows, HEAD_DIM), seq3),
                   pl.BlockSpec((dec_seq, LANES), lambda b, p, pt: (b, 0))])
    grid_spec = pltpu.PrefetchScalarGridSpec(
        num_scalar_prefetch=1,
        grid=(n_seq, n_steps),
        in_specs=in_specs,
        out_specs=pl.BlockSpec((dec_seq, width), lambda b, p, pt: (b, 0)),
        scratch_shapes=[pltpu.VMEM((1, LANES), F32), pltpu.VMEM((1, LANES), F32),
                        pltpu.VMEM((LANES, HEAD_DIM), F32), pltpu.VMEM((1, LANES), F32)])
    est = 2 * 2 * n_group * page_rows * HEAD_DIM * 4 + n_group * page_rows * LANES * 12
    return pl.pallas_call(
        functools.partial(_fox_sample_body, n_group=n_group, n_heads=n_heads, dec_seq=dec_seq),
        grid_spec=grid_spec,
        out_shape=jax.ShapeDtypeStruct((n_seq * dec_seq, width), F32),
        compiler_params=_params(est, 2),
        name="fox_sample",
    )(page_table, *([ck_rows] * n_group), *([cv_rows] * n_group), ck_pages, qt, kn_rows, vn_rows, lf_s)


def kernel(x_prompt, x_sample, cache_k, cache_v, cache_logf, page_table, g_pre, g_post, w_ffn_gate, w_ffn_up, w_ffn_down, w_a_in, a_ln_g, a_ln_b, w_a_spatial, b_a_spatial, w_a_out, g_kv, w_kvf, b_f, w_q, w_o):
    batch, seq, d_model = x_prompt.shape
    dec_batch, dec_seq, _ = x_sample.shape
    depth = g_pre.shape[0]
    n_a = w_a_in.shape[0]
    d_ff = w_ffn_gate.shape[-1]
    d_u = a_ln_g.shape[-1]
    n_heads = b_f.shape[0]
    width = n_heads * HEAD_DIM
    q_factor = LOG2E * HEAD_DIM ** -0.5

    x_p = x_prompt.reshape(batch * seq, d_model)
    x_s = x_sample.reshape(dec_batch * dec_seq, d_model)
    h_p, h_s = _prenorm(x_p, x_s, g_pre[0, 0])

    def ffn(x_p, x_s, h_p, h_s, l, j, g_next):
        a_p, a_s = _mm_ws(h_p, h_s, [(w_ffn_gate, (l, j), 0), (w_ffn_up, (l, j), 0)], d_ff,
                          _swiglu_epilogue, [(BF16, BF16)], name="ffn_gate_up")
        return _mm_res(a_p, a_s, w_ffn_down, (l, j), x_p, x_s, g_post[l, 2 * j], g_next,
                       0.5, bm=256, bk=512, name="ffn_down")

    sgu_p, sgu_s = [], []
    kv_out = None
    for l in range(depth):
        if l == n_a:
            k_p, k_s, kb_p, _, v_p, v_s, vb_p, _ = _mm_ws(
                hk_p, hk_s, [(w_kvf, (), 0), (w_kvf, (), width)], width, _kv_epilogue,
                [(F32, F32), (BF16, BF16), (F32, F32), (BF16, BF16)], name="proj_kv")
            lf_p, lf_s = _logf(x_p, x_s, g_kv, w_kvf[:, 2 * width:], b_f)
            c_p = _cumsum_rows(lf_p, batch)
            ck_pages = _ck_past(cache_logf, page_table, dec_seq)
            kv_out = (k_p, k_s, v_p, v_s, lf_p, lf_s)

        x_p, x_s, h_p, h_s = ffn(x_p, x_s, h_p, h_s, l, 0, g_pre[l, 1:2])

        if l < n_a:
            u_p, u_s, v_p_raw, v_s_raw = _mm_ws(
                h_p, h_s, [(w_a_in, (l,), 0), (w_a_in, (l,), d_u)], d_u, _gelu2_epilogue,
                [(BF16, BF16), (F32, F32)], name="sgu_in")
            m_p, m_s, vt_p, vt_s = _sgu_mix(u_p, v_p_raw, u_s, v_s_raw, a_ln_g[l], a_ln_b[l],
                                            w_a_spatial[l], b_a_spatial[l], batch, dec_batch)
            sgu_p.append(vt_p.reshape(batch, CHUNK, d_u))
            sgu_s.append(vt_s.reshape(dec_batch, dec_seq, d_u))
            w_mix, lead, bk = w_a_out, (l,), 512
        else:
            b = l - n_a
            q_p, q_s = _mm_ws(h_p, h_s, [(w_q, (b,), 0)], width,
                              functools.partial(_scaled_epilogue, factor=q_factor), [(BF16, F32)],
                              bn=1024, name="proj_q")
            m_p = _fox_prompt(q_p, kb_p, vb_p, c_p, batch)
            m_s = _fox_sample(q_s, k_s, v_s, lf_s, cache_k, cache_v, ck_pages, page_table, dec_seq)
            w_mix, lead, bk = w_o, (b,), 512
        x_p, x_s, h_p, h_s = _mm_res(m_p, m_s, w_mix, lead, x_p, x_s, g_post[l, 1], g_pre[l, 2:3], 1.0,
                                     bm=256 if l < n_a else 512, bk=bk, name="mixer_out")

        if l + 1 < depth:
            g_next = g_pre[l + 1, 0:1]
            if l + 1 == n_a:
                g_next = jnp.concatenate([g_next, g_kv.reshape(1, d_model)], axis=0)
            outs = ffn(x_p, x_s, h_p, h_s, l, 1, g_next)
            x_p, x_s, h_p, h_s = outs[:4]
            if l + 1 == n_a:
                hk_p, hk_s = outs[4:6]
        else:
            x_p, x_s = ffn(x_p, x_s, h_p, h_s, l, 1, None)

    k_p, k_s, v_p, v_s, lf_p, lf_s = kv_out
    return (x_p.reshape(batch, seq, d_model),
            x_s.reshape(dec_batch, dec_seq, d_model),
            k_p.reshape(batch, seq, n_heads, HEAD_DIM),
            v_p.reshape(batch, seq, n_heads, HEAD_DIM),
            lf_p[:, :n_heads].reshape(batch, seq, n_heads),
            k_s.reshape(dec_batch, dec_seq, n_heads, HEAD_DIM),
            v_s.reshape(dec_batch, dec_seq, n_heads, HEAD_DIM),
            lf_s[:, :n_heads].reshape(dec_batch, dec_seq, n_heads),
            jnp.stack(sgu_p),
            jnp.stack(sgu_s))
```

```python
import functools

import jax
import jax.numpy as jnp
from jax import lax
from jax.experimental import pallas as pl
from jax.experimental.pallas import tpu as pltpu

F32 = jnp.float32
BF16 = jnp.bfloat16

RMS_EPS = 1e-6
LN_EPS = 1e-5
NEG_INF = -1e30
LOG2E = 1.4426950408889634
CHUNK = 128
N_GROUPS = 16
HEAD_DIM = 128
PAGE = 128

V7X_VMEM_LIMIT_CAP = 56 * 1024 * 1024
LANES = 128


def _vmem_limit(estimate_bytes):
    return int(min(max(estimate_bytes * 5 // 4, 16 * 1024 * 1024), V7X_VMEM_LIMIT_CAP))


def _params(estimate_bytes, n_grid_dims):
    return pltpu.CompilerParams(
        dimension_semantics=("arbitrary",) * n_grid_dims,
        vmem_limit_bytes=_vmem_limit(estimate_bytes))


def _rms(x, g):
    return x * lax.rsqrt(jnp.mean(x * x, axis=-1, keepdims=True) + RMS_EPS) * g


def _dot(a, b):
    return jnp.dot(a, b, preferred_element_type=F32)


def _dot_exact(a, b):
    return jnp.dot(a, b, preferred_element_type=F32, precision=lax.Precision.HIGHEST)


def _weight_spec(w, lead, block, index):
    n_lead = len(lead)
    assert w.ndim == n_lead + 2
    return pl.BlockSpec((None,) * n_lead + block, lambda *g: tuple(lead) + index(*g))


def _prenorm_body(xp_ref, xs_ref, g_ref, hp_ref, hs_ref, *, n_tiles):
    i = pl.program_id(0)

    @pl.when(i < n_tiles)
    def _():
        hp_ref[...] = _rms(xp_ref[...], g_ref[...]).astype(BF16)

    @pl.when(i == n_tiles)
    def _():
        hs_ref[...] = _rms(xs_ref[...], g_ref[...]).astype(BF16)


def _prenorm(x_p, x_s, g, bm=1024):
    mp, d = x_p.shape
    ms = x_s.shape[0]
    n_tiles = mp // bm
    last = n_tiles - 1
    row = lambda i: (jnp.minimum(i, last), 0)
    fixed = lambda i: (0, 0)
    return pl.pallas_call(
        functools.partial(_prenorm_body, n_tiles=n_tiles),
        grid=(n_tiles + 1,),
        in_specs=[pl.BlockSpec((bm, d), row), pl.BlockSpec((ms, d), fixed),
                  pl.BlockSpec((1, d), fixed)],
        out_specs=[pl.BlockSpec((bm, d), row), pl.BlockSpec((ms, d), fixed)],
        out_shape=[jax.ShapeDtypeStruct((mp, d), BF16), jax.ShapeDtypeStruct((ms, d), BF16)],
        compiler_params=_params(6 * bm * d * 4, 1),
        name="prenorm",
    )(x_p, x_s, g.reshape(1, d))


def _mm_ws_body(*refs, n_w, n_out, n_tiles, epilogue):
    hp_ref, hs_ref = refs[0], refs[1]
    w_refs = refs[2:2 + n_w]
    out_refs = refs[2 + n_w:2 + n_w + 2 * n_out]
    wbf_refs = refs[2 + n_w + 2 * n_out:]
    m = pl.program_id(1)

    def run(h_ref, o_refs):
        h = h_ref[...]
        outs = epilogue(*[_dot(h, wbf_ref[...]) for wbf_ref in wbf_refs])
        for o_ref, o in zip(o_refs, outs):
            o_ref[...] = o.astype(o_ref.dtype)

    @pl.when(m == 0)
    def _():
        for w_ref, wbf_ref in zip(w_refs, wbf_refs):
            wbf_ref[...] = w_ref[...].astype(BF16)

    run(hp_ref, out_refs[0::2])

    @pl.when(m == n_tiles - 1)
    def _():
        run(hs_ref, out_refs[1::2])


def _mm_ws(h_p, h_s, weights, n_cols, epilogue, out_dtypes, *, bm=1024, bn=512, name):
    mp, k = h_p.shape
    ms = h_s.shape[0]
    n_tiles = mp // bm
    n_w, n_out = len(weights), len(out_dtypes)
    in_specs = [pl.BlockSpec((bm, k), lambda n, m: (m, 0)),
                pl.BlockSpec((ms, k), lambda n, m: (0, 0))]
    for w, lead, off in weights:
        assert off % bn == 0
        in_specs.append(_weight_spec(w, lead, (k, bn), lambda n, m, o=off // bn: (0, n + o)))
    out_specs, out_shape = [], []
    for dt_p, dt_s in out_dtypes:
        out_specs += [pl.BlockSpec((bm, bn), lambda n, m: (m, n)),
                      pl.BlockSpec((ms, bn), lambda n, m: (0, n))]
        out_shape += [jax.ShapeDtypeStruct((mp, n_cols), dt_p),
                      jax.ShapeDtypeStruct((ms, n_cols), dt_s)]
    est = (2 * bm * k * 2 + n_w * k * bn * (2 * 4 + 2)
           + n_out * 2 * bm * bn * 4 + (n_w + n_out) * bm * bn * 4)
    return pl.pallas_call(
        functools.partial(_mm_ws_body, n_w=n_w, n_out=n_out, n_tiles=n_tiles, epilogue=epilogue),
        grid=(n_cols // bn, n_tiles),
        in_specs=in_specs, out_specs=out_specs, out_shape=out_shape,
        scratch_shapes=[pltpu.VMEM((k, bn), BF16) for _ in weights],
        compiler_params=_params(est, 2),
        name=name,
    )(h_p, h_s, *[w for w, _, _ in weights])


def _swiglu_epilogue(g, u):
    return (g / (1.0 + jnp.exp(-g)) * u,)


def _gelu(z):
    return 0.5 * z * (1.0 + lax.erf(z * (2.0 ** -0.5)))


def _gelu2_epilogue(zu, zv):
    return (_gelu(zu), _gelu(zv))


def _kv_epilogue(zk, zv):
    return (zk, zk, zv, zv)


def _scaled_epilogue(z, *, factor):
    return (z * factor,)


def _mm_res_body(*refs, n_h, n_k, bk, n_tiles, coef):
    ap_ref, as_ref, w_ref, xp_ref, xs_ref, gpost_ref, gnext_ref = refs[:7]
    out_refs = refs[7:7 + 2 * (1 + n_h)]
    wbf_ref, y0_ref = refs[7 + 2 * (1 + n_h):]
    i = pl.program_id(0)

    @pl.when(i < n_k)
    def _():
        r0 = pl.multiple_of(i * bk, bk)
        w_chunk = w_ref[...].astype(BF16)
        wbf_ref[pl.ds(r0, bk), :] = w_chunk
        part = _dot(ap_ref[:, pl.ds(r0, bk)].astype(BF16), w_chunk)

        @pl.when(i == 0)
        def _():
            y0_ref[...] = part

        @pl.when(i > 0)
        def _():
            y0_ref[...] += part

    def run(a_ref, x_ref, o_refs, y=None):
        if y is None:
            y = _dot(a_ref[...].astype(BF16), wbf_ref[...])
        x_new = x_ref[...] + coef * _rms(y, gpost_ref[...])
        o_refs[0][...] = x_new
        if n_h:
            xn = x_new * lax.rsqrt(jnp.mean(x_new * x_new, axis=-1, keepdims=True) + RMS_EPS)
            for j in range(n_h):
                o_refs[1 + j][...] = (xn * gnext_ref[j:j + 1, :]).astype(BF16)

    @pl.when(i == n_k)
    def _():
        run(ap_ref, xp_ref, out_refs[0::2], y0_ref[...])

    @pl.when(jnp.logical_and(i > n_k, i < n_k + n_tiles))
    def _():
        run(ap_ref, xp_ref, out_refs[0::2])

    @pl.when(i == n_k + n_tiles)
    def _():
        run(as_ref, xs_ref, out_refs[1::2])


def _mm_res(a_p, a_s, w, lead, x_p, x_s, g_post, g_next, coef, *, bm, bk, name):
    mp, k = a_p.shape
    ms = a_s.shape[0]
    d = w.shape[-1]
    n_h = 0 if g_next is None else g_next.shape[0]
    g_next_arr = jnp.zeros((1, d), F32) if g_next is None else g_next
    n_tiles = mp // bm
    n_k = k // bk
    assert n_k * bk == k
    row = lambda i: (jnp.clip(i - n_k, 0, n_tiles - 1), 0)
    fixed = lambda i: (0, 0)
    in_specs = [pl.BlockSpec((bm, k), row), pl.BlockSpec((ms, k), fixed),
                _weight_spec(w, lead, (bk, d), lambda i: (jnp.minimum(i, n_k - 1), 0)),
                pl.BlockSpec((bm, d), row), pl.BlockSpec((ms, d), fixed),
                pl.BlockSpec((1, d), fixed), pl.BlockSpec(g_next_arr.shape, fixed)]
    out_specs = [pl.BlockSpec((bm, d), row), pl.BlockSpec((ms, d), fixed)]
    out_shape = [jax.ShapeDtypeStruct((mp, d), F32), jax.ShapeDtypeStruct((ms, d), F32)]
    for _ in range(n_h):
        out_specs += [pl.BlockSpec((bm, d), row), pl.BlockSpec((ms, d), fixed)]
        out_shape += [jax.ShapeDtypeStruct((mp, d), BF16), jax.ShapeDtypeStruct((ms, d), BF16)]
    est = (k * d * 2 + 2 * bk * d * 4 + 2 * bm * k * a_p.dtype.itemsize + 4 * bm * d * 4
           + n_h * 2 * bm * d * 2 + 3 * bm * d * 4)
    return pl.pallas_call(
        functools.partial(_mm_res_body, n_h=n_h, n_k=n_k, bk=bk, n_tiles=n_tiles, coef=coef),
        grid=(n_k + n_tiles + 1,),
        in_specs=in_specs, out_specs=out_specs, out_shape=out_shape,
        scratch_shapes=[pltpu.VMEM((k, d), BF16), pltpu.VMEM((bm, d), F32)],
        compiler_params=_params(est, 1),
        name=name,
    )(a_p, a_s, w, x_p, x_s, g_post.reshape(1, d), g_next_arr)


def _sgu_body(up_ref, vp_ref, us_ref, vs_ref, lng_ref, lnb_ref, wp_ref, bp_ref, ws_ref, bs_ref,
              pp_ref, ps_ref, vtp_ref, vts_ref, *, n_steps, cps):
    c = pl.program_id(0)

    def run(u_ref, v_ref, w_ref, b_ref, p_ref, vt_ref, n_sub):
        rows = v_ref.shape[0] // n_sub
        gw = v_ref.shape[1] // N_GROUPS
        causal = (lax.broadcasted_iota(jnp.int32, (rows, rows), 0)
                  >= lax.broadcasted_iota(jnp.int32, (rows, rows), 1))
        for sub in range(n_sub):
            r = slice(sub * rows, (sub + 1) * rows)
            v = v_ref[r, :]
            xc = v - jnp.mean(v, axis=-1, keepdims=True)
            var = jnp.mean(xc * xc, axis=-1, keepdims=True)
            vn = xc * lax.rsqrt(var + LN_EPS) * lng_ref[...] + lnb_ref[...]
            if sub == n_sub - 1:
                vt_ref[...] = vn
            vb = vn.astype(BF16)
            for g in range(N_GROUPS):
                wm = jnp.where(causal, w_ref[g], 0.0).astype(BF16)
                mixed = _dot(wm, vb[:, g * gw:(g + 1) * gw]) + b_ref[:, g:g + 1]
                p_ref[r, g * gw:(g + 1) * gw] = (
                    u_ref[r, g * gw:(g + 1) * gw].astype(F32) * mixed).astype(BF16)

    @pl.when(c < n_steps)
    def _():
        run(up_ref, vp_ref, wp_ref, bp_ref, pp_ref, vtp_ref, cps)

    @pl.when(c == n_steps)
    def _():
        run(us_ref, vs_ref, ws_ref, bs_ref, ps_ref, vts_ref, 1)


def _sgu_mix(u_p, v_p, u_s, v_s, ln_g, ln_b, w_sp, b_sp, batch, dec_batch, cps=4):
    mp, du = u_p.shape
    ms = u_s.shape[0]
    n_steps = mp // (cps * CHUNK)
    steps_per_seq = n_steps // batch
    assert steps_per_seq * batch * cps * CHUNK == mp
    dec_seq = ms // dec_batch
    last = n_steps - 1
    rows = cps * CHUNK
    eye = jnp.eye(dec_batch, dtype=F32)
    w_s = (eye[None, :, None, :, None] * w_sp[:, None, :dec_seq, None, :dec_seq]
           ).reshape(N_GROUPS, ms, ms)
    b_s = jnp.tile(jnp.transpose(b_sp[:, :dec_seq]), (dec_batch, 1))
    b_p = jnp.transpose(b_sp)
    row = lambda c: (jnp.minimum(c, last), 0)
    fixed2 = lambda c: (0, 0)
    fixed3 = lambda c: (0, 0, 0)
    est = 2 * rows * du * (2 + 4 + 2) + 2 * CHUNK * du * 4 + 8 * CHUNK * du * 4
    return pl.pallas_call(
        functools.partial(_sgu_body, n_steps=n_steps, cps=cps),
        grid=(n_steps + 1,),
        in_specs=[pl.BlockSpec((rows, du), row), pl.BlockSpec((rows, du), row),
                  pl.BlockSpec((ms, du), fixed2), pl.BlockSpec((ms, du), fixed2),
                  pl.BlockSpec((1, du), fixed2), pl.BlockSpec((1, du), fixed2),
                  pl.BlockSpec((N_GROUPS, CHUNK, CHUNK), fixed3), pl.BlockSpec((CHUNK, N_GROUPS), fixed2),
                  pl.BlockSpec((N_GROUPS, ms, ms), fixed3), pl.BlockSpec((ms, N_GROUPS), fixed2)],
        out_specs=[pl.BlockSpec((rows, du), row), pl.BlockSpec((ms, du), fixed2),
                   pl.BlockSpec((CHUNK, du), lambda c: (jnp.minimum(c, last) // steps_per_seq, 0)),
                   pl.BlockSpec((ms, du), fixed2)],
        out_shape=[jax.ShapeDtypeStruct((mp, du), BF16), jax.ShapeDtypeStruct((ms, du), BF16),
                   jax.ShapeDtypeStruct((batch * CHUNK, du), F32), jax.ShapeDtypeStruct((ms, du), F32)],
        compiler_params=_params(est, 1),
        name="sgu_mix",
    )(u_p, v_p, u_s, v_s, ln_g.reshape(1, du), ln_b.reshape(1, du), w_sp, b_p, w_s, b_s)


def _logf_body(xp_ref, xs_ref, g_ref, w_ref, b_ref, lp_ref, ls_ref, *, n_tiles):
    i = pl.program_id(0)

    def run(x_ref, o_ref):
        z = _dot_exact(_rms(x_ref[...], g_ref[...]), w_ref[...]) + b_ref[...]
        o_ref[...] = jnp.minimum(z, 0.0) - jnp.log1p(jnp.exp(-jnp.abs(z)))

    @pl.when(i < n_tiles)
    def _():
        run(xp_ref, lp_ref)

    @pl.when(i == n_tiles)
    def _():
        run(xs_ref, ls_ref)


def _logf(x_p, x_s, g_kv, w_f, b_f, bm=512):
    mp, d = x_p.shape
    ms = x_s.shape[0]
    nh = w_f.shape[1]
    w_pad = jnp.pad(w_f, ((0, 0), (0, LANES - nh)))
    b_pad = jnp.pad(b_f, (0, LANES - nh)).reshape(1, LANES)
    n_tiles = mp // bm
    last = n_tiles - 1
    row = lambda i: (jnp.minimum(i, last), 0)
    fixed = lambda i: (0, 0)
    return pl.pallas_call(
        functools.partial(_logf_body, n_tiles=n_tiles),
        grid=(n_tiles + 1,),
        in_specs=[pl.BlockSpec((bm, d), row), pl.BlockSpec((ms, d), fixed),
                  pl.BlockSpec((1, d), fixed), pl.BlockSpec((d, LANES), fixed),
                  pl.BlockSpec((1, LANES), fixed)],
        out_specs=[pl.BlockSpec((bm, LANES), row), pl.BlockSpec((ms, LANES), fixed)],
        out_shape=[jax.ShapeDtypeStruct((mp, LANES), F32), jax.ShapeDtypeStruct((ms, LANES), F32)],
        compiler_params=_params(8 * bm * d * 4, 1),
        name="logf",
    )(x_p, x_s, g_kv.reshape(1, d), w_pad, b_pad)


def _cumsum_body(l_ref, c_ref, *, n_chunks):
    tril = (lax.broadcasted_iota(jnp.int32, (CHUNK, CHUNK), 0)
            >= lax.broadcasted_iota(jnp.int32, (CHUNK, CHUNK), 1)).astype(F32)

    def step(j, carry):
        r0 = pl.multiple_of(j * CHUNK, CHUNK)
        local = _dot_exact(tril, l_ref[pl.ds(r0, CHUNK), :]) + carry
        c_ref[pl.ds(r0, CHUNK), :] = local
        return local[CHUNK - 1:CHUNK, :]

    lax.fori_loop(0, n_chunks, step, jnp.zeros((1, LANES), F32))


def _cumsum_rows(lf_p, batch):
    mp = lf_p.shape[0]
    seq = mp // batch
    return pl.pallas_call(
        functools.partial(_cumsum_body, n_chunks=seq // CHUNK),
        grid=(batch,),
        in_specs=[pl.BlockSpec((seq, LANES), lambda b: (b, 0))],
        out_specs=pl.BlockSpec((seq, LANES), lambda b: (b, 0)),
        out_shape=jax.ShapeDtypeStruct((mp, LANES), F32),
        compiler_params=_params(4 * seq * LANES * 4, 1),
        name="cumsum_logf",
    )(lf_p)


def _head_column(c, h):
    lane = lax.broadcasted_iota(jnp.int32, (1, LANES), 1)
    return jnp.sum(jnp.where(lane == h, c, 0.0), axis=-1, keepdims=True) * LOG2E


def _split3(c):
    c1 = c.astype(BF16).astype(F32)
    r1 = c - c1
    c2 = r1.astype(BF16).astype(F32)
    c3 = (r1 - c2).astype(BF16).astype(F32)
    return c1, c2, c3


def _bias_columns(c_col, query_side):
    rows = c_col.shape[0]
    lane = lax.broadcasted_iota(jnp.int32, (rows, LANES), 1)
    c1, c2, c3 = _split3(c_col)
    if query_side:
        aug = jnp.where(lane == 0, c1, jnp.where(lane == 1, c2, jnp.where(lane == 2, c3, jnp.where(lane < 6, 1.0, 0.0))))
    else:
        aug = jnp.where(lane < 3, 1.0, jnp.where(lane == 3, -c1, jnp.where(lane == 4, -c2, jnp.where(lane == 5, -c3, 0.0))))
    return aug.astype(BF16)


def _fox_prompt_body(q_ref, k_ref, v_ref, cq_ref, ck_ref, o_ref, kaug_ref, vt_ref, *, tq, tk, hs, nq):
    hg = pl.program_id(1)
    qi = pl.program_id(2)
    head_cols = [slice(a * HEAD_DIM, (a + 1) * HEAD_DIM) for a in range(hs)]
    aug_cols = [slice(a * 2 * HEAD_DIM, (a + 1) * 2 * HEAD_DIM) for a in range(hs)]

    @pl.when(qi == 0)
    def _():
        ck = ck_ref[...]
        for a in range(hs):
            kaug_ref[:, aug_cols[a]] = jnp.concatenate(
                [k_ref[:, head_cols[a]], _bias_columns(_head_column(ck, hg * hs + a), False)], axis=1)
            vt_ref[a] = v_ref[:, head_cols[a]].astype(F32).T.astype(BF16)

    cq = cq_ref[...]
    q_t = [jnp.concatenate([q_ref[:, head_cols[a]].astype(F32),
                            _bias_columns(_head_column(cq, hg * hs + a), True).astype(F32)],
                           axis=1).T.astype(BF16) for a in range(hs)]

    def tile(j, carry, masked):
        r0 = j * tk
        out = []
        for a in range(hs):
            m, l, acc = carry[a]
            s = _dot(kaug_ref[pl.ds(r0, tk), aug_cols[a]], q_t[a])
            if masked:
                s = jnp.where(lax.broadcasted_iota(jnp.int32, (tk, tq), 0)
                              <= lax.broadcasted_iota(jnp.int32, (tk, tq), 1), s, NEG_INF)
            m_new = jnp.maximum(m, jnp.max(s, axis=0, keepdims=True))
            alpha = jnp.exp2(m - m_new)
            p = jnp.exp2(s - m_new)
            l = alpha * l + jnp.sum(p, axis=0, keepdims=True)
            acc = alpha * acc + _dot(vt_ref[a, :, pl.ds(r0, tk)], p.astype(BF16))
            out.append((m_new, l, acc))
        return tuple(out)

    for n_full in range(nq):
        @pl.when(qi == n_full)
        def _(n_full=n_full):
            carry = tuple((jnp.full((1, tq), NEG_INF, F32), jnp.zeros((1, tq), F32),
                           jnp.zeros((HEAD_DIM, tq), F32)) for _ in range(hs))
            for j in range(n_full):
                carry = tile(j, carry, False)
            carry = tile(n_full, carry, True)
            for a in range(hs):
                _, l, acc = carry[a]
                o_ref[:, head_cols[a]] = (acc / l).T.astype(o_ref.dtype)


def _fox_prompt(q_p, k_bf, v_bf, c_p, batch, tq=1024, tk=1024, hs=2):
    assert tq == tk
    mp, width = q_p.shape
    n_heads = width // HEAD_DIM
    seq = mp // batch
    nq = seq // tq
    hw = hs * HEAD_DIM
    est = 2 * (2 * seq * hw * 2) + 3 * seq * LANES * 4 + 3 * seq * hw * 2 + hs * 6 * tq * tk * 4
    return pl.pallas_call(
        functools.partial(_fox_prompt_body, tq=tq, tk=tk, hs=hs, nq=nq),
        grid=(batch, n_heads // hs, nq),
        in_specs=[pl.BlockSpec((tq, hw), lambda b, h, i: (b * nq + i, h)),
                  pl.BlockSpec((seq, hw), lambda b, h, i: (b, h)),
                  pl.BlockSpec((seq, hw), lambda b, h, i: (b, h)),
                  pl.BlockSpec((tq, LANES), lambda b, h, i: (b * nq + i, 0)),
                  pl.BlockSpec((seq, LANES), lambda b, h, i: (b, 0))],
        out_specs=pl.BlockSpec((tq, hw), lambda b, h, i: (b * nq + i, h)),
        out_shape=jax.ShapeDtypeStruct((mp, width), BF16),
        scratch_shapes=[pltpu.VMEM((seq, 2 * hw), BF16), pltpu.VMEM((hs, HEAD_DIM, seq), BF16)],
        compiler_params=_params(est, 3),
        name="fox_prompt",
    )(q_p, k_bf, v_bf, c_p, c_p)


def _ck_past_body(pt_ref, *refs, n_group, n_heads, dec_seq):
    lf_refs = refs[:n_group]
    out_ref, carry_ref = refs[n_group:]
    p = pl.program_id(1)

    @pl.when(p == 0)
    def _():
        carry_ref[...] = jnp.zeros_like(carry_ref)

    expand = (lax.broadcasted_iota(jnp.int32, (n_heads, LANES), 1) // dec_seq
              == lax.broadcasted_iota(jnp.int32, (n_heads, LANES), 0)).astype(BF16)
    later = (lax.broadcasted_iota(jnp.int32, (PAGE, PAGE), 1)
             > lax.broadcasted_iota(jnp.int32, (PAGE, PAGE), 0)).astype(BF16)
    lf = jnp.concatenate([lf_refs[g][0] for g in range(n_group)], axis=0)
    lf_e, within = None, None
    for term in _split3(lf):
        e = _dot(term.astype(BF16), expand)
        e_by_lanes = jnp.concatenate([e[g * PAGE:(g + 1) * PAGE] for g in range(n_group)], axis=1)
        w = _dot(later, e_by_lanes.astype(BF16))
        lf_e = e if lf_e is None else lf_e + e
        within = w if within is None else within + w
    carry = carry_ref[...]
    for g in range(n_group):
        within_g = within[:, g * LANES:(g + 1) * LANES]
        out_ref[0, n_group - 1 - g] = -(within_g + carry)
        carry = carry + within_g[0:1, :] + lf_e[g * PAGE:g * PAGE + 1, :]
    carry_ref[...] = carry


def _ck_past(cache_logf, page_table, dec_seq, n_group=64):
    n_seq, n_pages = page_table.shape
    n_heads = cache_logf.shape[2]
    assert n_heads * dec_seq == LANES and n_pages % n_group == 0
    lf_map = lambda b, p, pt, g: (pt[b, n_pages - 1 - (p * n_group + g)], 0, 0)
    n_steps = n_pages // n_group
    grid_spec = pltpu.PrefetchScalarGridSpec(
        num_scalar_prefetch=1,
        grid=(n_seq, n_steps),
        in_specs=[pl.BlockSpec((1, PAGE, n_heads), functools.partial(lf_map, g=g))
                  for g in range(n_group)],
        out_specs=pl.BlockSpec((1, n_group, PAGE, LANES), lambda b, p, pt: (b, n_steps - 1 - p, 0, 0)),
        scratch_shapes=[pltpu.VMEM((1, LANES), F32)])
    return pl.pallas_call(
        functools.partial(_ck_past_body, n_group=n_group, n_heads=n_heads, dec_seq=dec_seq),
        grid_spec=grid_spec,
        out_shape=jax.ShapeDtypeStruct((n_seq, n_pages, PAGE, LANES), F32),
        compiler_params=_params((16 * n_group + 64) * PAGE * LANES * 4, 2),
        name="ck_past",
    )(page_table, *([cache_logf] * n_group))


def _fox_sample_body(pt_ref, *refs, n_group, n_heads, dec_seq):
    k_refs = refs[:n_group]
    v_refs = refs[n_group:2 * n_group]
    ck_ref, qt_ref, kn_ref, vn_ref, lfn_ref, o_ref, m_ref, l_ref, acc_ref, cq_ref = refs[2 * n_group:]
    p = pl.program_id(1)
    n_steps = pl.num_programs(1)
    qt = qt_ref[0]
    eye = (lax.broadcasted_iota(jnp.int32, (LANES, LANES), 0)
           == lax.broadcasted_iota(jnp.int32, (LANES, LANES), 1))
    head_match = (lax.broadcasted_iota(jnp.int32, (n_heads, LANES), 1) // dec_seq
                  == lax.broadcasted_iota(jnp.int32, (n_heads, LANES), 0))

    def to_column(row):
        return jnp.sum(jnp.where(eye, row, 0.0), axis=-1, keepdims=True)

    def scores(rows_ref, bias, keep):
        s = _dot(rows_ref[0].astype(BF16), qt)
        s = s.reshape(bias.shape[0], n_heads, LANES) + bias[:, None, :]
        return jnp.where(keep, s, NEG_INF)

    def col_max(s3):
        return jnp.max(jnp.max(s3, axis=0), axis=0, keepdims=True)

    def col_sum(p3):
        return jnp.sum(jnp.sum(p3, axis=0), axis=0, keepdims=True)

    def weighted_values(p3, rows_ref):
        pr = p3.reshape(p3.shape[0] * n_heads, LANES).astype(BF16)
        return lax.dot_general(pr, rows_ref[0].astype(BF16), (((0,), (0,)), ((), ())),
                               preferred_element_type=F32)

    @pl.when(p == 0)
    def _():
        expand = (lax.broadcasted_iota(jnp.int32, (LANES, LANES), 1) // dec_seq
                  == lax.broadcasted_iota(jnp.int32, (LANES, LANES), 0)).astype(F32)
        lf_e = _dot_exact(lfn_ref[...], expand)
        rows = [lf_e[0:1, :]]
        for j in range(1, dec_seq):
            rows.append(rows[-1] + lf_e[j:j + 1, :])
        c_new = jnp.concatenate(rows, axis=0)
        key_j = lax.broadcasted_iota(jnp.int32, (dec_seq, LANES), 0)
        col_t = lax.broadcasted_iota(jnp.int32, (dec_seq, LANES), 1) % dec_seq
        cq = jnp.sum(jnp.where(key_j == col_t, c_new, 0.0), axis=0, keepdims=True)
        cq_ref[...] = cq
        causal = (key_j <= col_t)[:, None, :]
        s3 = scores(kn_ref, (cq - c_new) * LOG2E, jnp.logical_and(causal, head_match[None]))
        m = col_max(s3)
        p3 = jnp.exp2(s3 - m[None])
        m_ref[...] = m
        l_ref[...] = col_sum(p3)
        acc_ref[...] = weighted_values(p3, vn_ref)

    cq = cq_ref[...]
    s_pages = [scores(k_refs[g], (cq - ck_ref[0, n_group - 1 - g]) * LOG2E, head_match[None])
               for g in range(n_group)]
    m_old = m_ref[...]
    m_new = m_old
    for s3 in s_pages:
        m_new = jnp.maximum(m_new, col_max(s3))
    alpha = jnp.exp2(m_old - m_new)
    l_new = alpha * l_ref[...]
    pv = None
    for g in range(n_group):
        p3 = jnp.exp2(s_pages[g] - m_new[None])
        l_new = l_new + col_sum(p3)
        t = weighted_values(p3, v_refs[g])
        pv = t if pv is None else pv + t
    m_ref[...] = m_new
    l_ref[...] = l_new
    acc_ref[...] = to_column(alpha) * acc_ref[...] + pv

    @pl.when(p == n_steps - 1)
    def _():
        out = acc_ref[...] / to_column(l_ref[...])
        for h in range(n_heads):
            o_ref[:, h * HEAD_DIM:(h + 1) * HEAD_DIM] = out[h * dec_seq:(h + 1) * dec_seq, :]


def _fox_sample(q_s, k_s, v_s, lf_s, cache_k, cache_v, ck_pages, page_table, dec_seq, n_group=8):
    n_seq, n_pages = page_table.shape
    n_heads = cache_k.shape[2]
    width = n_heads * HEAD_DIM
    page_rows = PAGE * n_heads
    new_rows = dec_seq * n_heads
    n_steps = n_pages // n_group
    ck_rows = cache_k.reshape(cache_k.shape[0], page_rows, HEAD_DIM)
    cv_rows = cache_v.reshape(cache_v.shape[0], page_rows, HEAD_DIM)
    kn_rows = k_s.reshape(n_seq, new_rows, HEAD_DIM)
    vn_rows = v_s.reshape(n_seq, new_rows, HEAD_DIM)
    qt = jnp.transpose(q_s.reshape(n_seq, dec_seq, n_heads, HEAD_DIM), (0, 3, 2, 1))
    qt = qt.reshape(n_seq, HEAD_DIM, n_heads * dec_seq).astype(BF16)
    page_map = lambda b, p, pt, g: (pt[b, n_pages - 1 - (p * n_group + g)], 0, 0)
    seq3 = lambda b, p, pt: (b, 0, 0)
    page_block = (1, page_rows, HEAD_DIM)
    in_specs = ([pl.BlockSpec(page_block, functools.partial(page_map, g=g)) for g in range(n_group)]
                + [pl.BlockSpec(page_block, functools.partial(page_map, g=g)) for g in range(n_group)]
                + [pl.BlockSpec((1, n_group, PAGE, LANES), lambda b, p, pt: (b, n_steps - 1 - p, 0, 0)),
                   pl.BlockSpec((1, HEAD_DIM, LANES), seq3),
                   pl.BlockSpec((1, new_rows, HEAD_DIM), seq3), pl.BlockSpec((1, new_rows, HEAD_DIM), seq3),
                   pl.BlockSpec((dec_seq, LANES), lambda b, p, pt: (b, 0))])
    grid_spec = pltpu.PrefetchScalarGridSpec(
        num_scalar_prefetch=1,
        grid=(n_seq, n_steps),
        in_specs=in_specs,
        out_specs=pl.BlockSpec((dec_seq, width), lambda b, p, pt: (b, 0)),
        scratch_shapes=[pltpu.VMEM((1, LANES), F32), pltpu.VMEM((1, LANES), F32),
                        pltpu.VMEM((LANES, HEAD_DIM), F32), pltpu.VMEM((1, LANES), F32)])
    est = 2 * 2 * n_group * page_rows * HEAD_DIM * 4 + n_group * page_rows * LANES * 12
    return pl.pallas_call(
        functools.partial(_fox_sample_body, n_group=n_group, n_heads=n_heads, dec_seq=dec_seq),
        grid_spec=grid_spec,
        out_shape=jax.ShapeDtypeStruct((n_seq * dec_seq, width), F32),
        compiler_params=_params(est, 2),
        name="fox_sample",
    )(page_table, *([ck_rows] * n_group), *([cv_rows] * n_group), ck_pages, qt, kn_rows, vn_rows, lf_s)


def kernel(x_prompt, x_sample, cache_k, cache_v, cache_logf, page_table, g_pre, g_post, w_ffn_gate, w_ffn_up, w_ffn_down, w_a_in, a_ln_g, a_ln_b, w_a_spatial, b_a_spatial, w_a_out, g_kv, w_kvf, b_f, w_q, w_o):
    batch, seq, d_model = x_prompt.shape
    dec_batch, dec_seq, _ = x_sample.shape
    depth = g_pre.shape[0]
    n_a = w_a_in.shape[0]
    d_ff = w_ffn_gate.shape[-1]
    d_u = a_ln_g.shape[-1]
    n_heads = b_f.shape[0]
    width = n_heads * HEAD_DIM
    q_factor = LOG2E * HEAD_DIM ** -0.5

    x_p = x_prompt.reshape(batch * seq, d_model)
    x_s = x_sample.reshape(dec_batch * dec_seq, d_model)
    h_p, h_s = _prenorm(x_p, x_s, g_pre[0, 0])

    def ffn(x_p, x_s, h_p, h_s, l, j, g_next):
        a_p, a_s = _mm_ws(h_p, h_s, [(w_ffn_gate, (l, j), 0), (w_ffn_up, (l, j), 0)], d_ff,
                          _swiglu_epilogue, [(BF16, BF16)], name="ffn_gate_up")
        return _mm_res(a_p, a_s, w_ffn_down, (l, j), x_p, x_s, g_post[l, 2 * j], g_next,
                       0.5, bm=256, bk=512, name="ffn_down")

    sgu_p, sgu_s = [], []
    kv_out = None
    for l in range(depth):
        if l == n_a:
            k_p, k_s, kb_p, _, v_p, v_s, vb_p, _ = _mm_ws(
                hk_p, hk_s, [(w_kvf, (), 0), (w_kvf, (), width)], width, _kv_epilogue,
                [(F32, F32), (BF16, BF16), (F32, F32), (BF16, BF16)], name="proj_kv")
            lf_p, lf_s = _logf(x_p, x_s, g_kv, w_kvf[:, 2 * width:], b_f)
            c_p = _cumsum_rows(lf_p, batch)
            ck_pages = _ck_past(cache_logf, page_table, dec_seq)
            kv_out = (k_p, k_s, v_p, v_s, lf_p, lf_s)

        x_p, x_s, h_p, h_s = ffn(x_p, x_s, h_p, h_s, l, 0, g_pre[l, 1:2])

        if l < n_a:
            u_p, u_s, v_p_raw, v_s_raw = _mm_ws(
                h_p, h_s, [(w_a_in, (l,), 0), (w_a_in, (l,), d_u)], d_u, _gelu2_epilogue,
                [(BF16, BF16), (F32, F32)], name="sgu_in")
            m_p, m_s, vt_p, vt_s = _sgu_mix(u_p, v_p_raw, u_s, v_s_raw, a_ln_g[l], a_ln_b[l],
                                            w_a_spatial[l], b_a_spatial[l], batch, dec_batch)
            sgu_p.append(vt_p.reshape(batch, CHUNK, d_u))
            sgu_s.append(vt_s.reshape(dec_batch, dec_seq, d_u))
            w_mix, lead, bk = w_a_out, (l,), 512
        else:
            b = l - n_a
            q_p, q_s = _mm_ws(h_p, h_s, [(w_q, (b,), 0)], width,
                              functools.partial(_scaled_epilogue, factor=q_factor), [(BF16, F32)],
                              bn=1024, name="proj_q")
            m_p = _fox_prompt(q_p, kb_p, vb_p, c_p, batch)
            m_s = _fox_sample(q_s, k_s, v_s, lf_s, cache_k, cache_v, ck_pages, page_table, dec_seq)
            w_mix, lead, bk = w_o, (b,), 512
        x_p, x_s, h_p, h_s = _mm_res(m_p, m_s, w_mix, lead, x_p, x_s, g_post[l, 1], g_pre[l, 2:3], 1.0,
                                     bm=256 if l < n_a else 512, bk=bk, name="mixer_out")

        if l + 1 < depth:
            g_next = g_pre[l + 1, 0:1]
            if l + 1 == n_a:
                g_next = jnp.concatenate([g_next, g_kv.reshape(1, d_model)], axis=0)
            outs = ffn(x_p, x_s, h_p, h_s, l, 1, g_next)
            x_p, x_s, h_p, h_s = outs[:4]
            if l + 1 == n_a:
                hk_p, hk_s = outs[4:6]
        else:
            x_p, x_s = ffn(x_p, x_s, h_p, h_s, l, 1, None)

    k_p, k_s, v_p, v_s, lf_p, lf_s = kv_out
    return (x_p.reshape(batch, seq, d_model),
            x_s.reshape(dec_batch, dec_seq, d_model),
            k_p.reshape(batch, seq, n_heads, HEAD_DIM),
            v_p.reshape(batch, seq, n_heads, HEAD_DIM),
            lf_p[:, :n_heads].reshape(batch, seq, n_heads),
            k_s.reshape(dec_batch, dec_seq, n_heads, HEAD_DIM),
            v_s.reshape(dec_batch, dec_seq, n_heads, HEAD_DIM),
            lf_s[:, :n_heads].reshape(dec_batch, dec_seq, n_heads),
            jnp.stack(sgu_p),
            jnp.stack(sgu_s))
```
